```python
import jax, jax.numpy as jnp
from jax import lax
import numpy as np

D_MODEL = 1024
BATCH = 8
SEQ = 4096
DEPTH = 4

GRID_W = 64
CTX_LEN = 256
D_FF = 2816
N_SUB = 3
N_MOD = 3 * N_SUB
FFN_HALF = 0.5
NORM_EPS = 1e-6
ROPE_THETA = 10000.0
DEEP_ALPHA = (2 * DEPTH) ** 0.25
DEEP_BETA = (8 * DEPTH) ** -0.25

A_HEADS = 8
A_KV = 2
A_HD = 64
WINDOW = 128
BLOCK = 128
B_HEADS = 4
B_DK = 128
B_DV = 128
CONV_K = 5
CHUNK = 64
C_HEADS = 8
C_KV = 2
C_HD = 128
Q_BLOCK = 128

N_EVEN = (DEPTH + 1) // 2
N_ODD = DEPTH // 2
A_Q = A_HEADS * A_HD
A_KVW = A_KV * A_HD
B_QK = B_HEADS * B_DK
B_VW = B_HEADS * B_DV
B_QKV = 2 * B_QK + B_VW
AB_SIZES = (A_Q, A_KVW, A_KVW, B_QKV, B_VW, B_HEADS, B_HEADS, B_HEADS, B_HEADS)
AB_IN = A_Q + 2 * A_KVW + B_QKV + B_VW + 4 * B_HEADS
AB_OUT = A_Q + B_VW
C_IN = C_HEADS * C_HD + 2 * C_KV * C_HD
C_OUT = C_HEADS * C_HD

kernel_name = 'hybrid_dit_window_delta_axial_gqa'


def _split(p, sizes):
    out, start = [], 0
    for s in sizes:
        out.append(p[..., start:start + s])
        start += s
    return out


def layer_norm(x, g, b):
    x32 = x.astype(jnp.float32)
    mu = jnp.mean(x32, axis=-1, keepdims=True)
    var = jnp.mean(jnp.square(x32 - mu), axis=-1, keepdims=True)
    return ((x32 - mu) * lax.rsqrt(var + NORM_EPS) * g + b).astype(x.dtype)


def rms_norm(x, g):
    x32 = x.astype(jnp.float32)
    return (x32 * lax.rsqrt(jnp.mean(jnp.square(x32), axis=-1, keepdims=True) + NORM_EPS) * g).astype(x.dtype)


def l2_normalize(x):
    x32 = x.astype(jnp.float32)
    return (x32 * lax.rsqrt(jnp.sum(jnp.square(x32), axis=-1, keepdims=True) + NORM_EPS)).astype(x.dtype)


def axial_rope(rows, head_dim):
    n_freq = head_dim // 4
    inv = ROPE_THETA ** (-jnp.arange(n_freq, dtype=jnp.float32) / n_freq)
    r, col = jnp.meshgrid(jnp.arange(rows, dtype=jnp.float32), jnp.arange(GRID_W, dtype=jnp.float32), indexing='ij')
    r, col = r.reshape(-1), col.reshape(-1)
    ang = jnp.concatenate([r[:, None] * inv, col[:, None] * inv], axis=-1)
    return jnp.cos(ang), jnp.sin(ang)


def apply_rope(x, cos, sin):
    half = x.shape[-1] // 2
    x1, x2 = x[..., :half], x[..., half:]
    cs = cos[None, :, None, :].astype(x.dtype)
    sn = sin[None, :, None, :].astype(x.dtype)
    return jnp.concatenate([x1 * cs - x2 * sn, x1 * sn + x2 * cs], axis=-1)


def swiglu(h, w_gu, w_down):
    gate, up = jnp.split(h @ w_gu, 2, axis=-1)
    return (jax.nn.silu(gate) * up) @ w_down


def short_conv(x, w):
    pad = CONV_K // 2
    return lax.conv_general_dilated(x, w[:, None, :].astype(x.dtype), window_strides=(1,), padding=[(pad, pad)],
                                    dimension_numbers=('NWC', 'WIO', 'NWC'), feature_group_count=x.shape[-1])


def ctx_attention(q, k, v, n_kv, sink=None):
    bsz, lq, h, d = q.shape
    g = h // n_kv
    qg = q.reshape(bsz, lq, n_kv, g, d)
    s = jnp.einsum('bqhgd,bkhd->bhgqk', qg, k).astype(jnp.float32) * d ** -0.5
    if sink is not None:
        sk = jnp.broadcast_to(sink.reshape(n_kv, g, 1, 1).astype(jnp.float32), (bsz, n_kv, g, lq, 1))
        s = jnp.concatenate([sk, s], axis=-1)
    p = jax.nn.softmax(s, axis=-1)
    if sink is not None:
        p = p[..., 1:]
    o = jnp.einsum('bhgqk,bkhd->bqhgd', p.astype(v.dtype), v)
    return o.reshape(bsz, lq, h * d)


def window_sink_attention(q, k, v, kc, vc, sink):
    bsz, t = q.shape[:2]
    nb = t // BLOCK
    g = A_HEADS // A_KV
    scale = A_HD ** -0.5
    qb = q.reshape(bsz, nb, BLOCK, A_KV, g, A_HD)

    def band(a):
        ap = jnp.pad(a, ((0, 0), (BLOCK, BLOCK), (0, 0), (0, 0)))
        ab = ap.reshape(bsz, nb + 2, BLOCK, A_KV, A_HD)
        return jnp.concatenate([ab[:, :-2], ab[:, 1:-1], ab[:, 2:]], axis=2)

    kb, vb = band(k), band(v)
    s_loc = jnp.einsum('bnqhgd,bnkhd->bnhgqk', qb, kb).astype(jnp.float32) * scale
    s_ctx = jnp.einsum('bnqhgd,bkhd->bnhgqk', qb, kc).astype(jnp.float32) * scale
    qpos = jnp.arange(nb)[:, None] * BLOCK + jnp.arange(BLOCK)[None, :]
    kpos = (jnp.arange(nb)[:, None] - 1) * BLOCK + jnp.arange(3 * BLOCK)[None, :]
    kp = kpos[:, None, :]
    valid = (jnp.abs(qpos[:, :, None] - kp) <= WINDOW) & (kp >= 0) & (kp < t)
    s_loc = jnp.where(valid[None, :, None, None], s_loc, -jnp.inf)
    sk = jnp.broadcast_to(sink.reshape(A_KV, g, 1, 1).astype(jnp.float32), (bsz, nb, A_KV, g, BLOCK, 1))
    p = jax.nn.softmax(jnp.concatenate([sk, s_loc, s_ctx], axis=-1), axis=-1).astype(v.dtype)
    p_loc, p_ctx = p[..., 1:1 + 3 * BLOCK], p[..., 1 + 3 * BLOCK:]
    o = jnp.einsum('bnhgqk,bnkhd->bnqhgd', p_loc, vb) + jnp.einsum('bnhgqk,bkhd->bnqhgd', p_ctx, vc)
    return o.reshape(bsz, t, A_HEADS * A_HD)


def gated_delta_chunked(q, k, v, beta, g, s0):
    out_dtype = v.dtype
    bsz, t, h, _ = q.shape
    dv = v.shape[-1]
    n = t // CHUNK

    def chunks(a):
        a = a.astype(jnp.float32).reshape((bsz, n, CHUNK) + a.shape[2:])
        return jnp.swapaxes(a, 2, 3)

    qc, kc, vc, bc = chunks(q), chunks(k), chunks(v), chunks(beta)
    gc = jnp.cumsum(chunks(g), axis=-1)
    idx = jnp.arange(CHUNK)
    incl = idx[:, None] >= idx[None, :]
    strict = idx[:, None] > idx[None, :]
    decay = jnp.exp(jnp.where(incl, gc[..., :, None] - gc[..., None, :], -jnp.inf))
    kb = kc * bc[..., None]
    lmat = jnp.where(strict, jnp.einsum('bnhid,bnhjd->bnhij', kb, kc) * decay, 0.0)
    rhs = jnp.concatenate([vc * bc[..., None], kb * jnp.exp(gc)[..., None]], axis=-1)
    uw = lax.linalg.triangular_solve(lmat, rhs, left_side=True, lower=True, unit_diagonal=True)
    u, w = uw[..., :dv], uw[..., dv:]
    attn = jnp.einsum('bnhid,bnhjd->bnhij', qc, kc) * decay
    qd = qc * jnp.exp(gc)[..., None]
    g_last = gc[..., -1]
    kt = kc * jnp.exp(g_last[..., None] - gc)[..., None]

    def step(s, xs):
        u_i, w_i, qd_i, at_i, kt_i, gl_i = xs
        v_new = u_i - jnp.einsum('bhcd,bhde->bhce', w_i, s)
        o_i = jnp.einsum('bhcd,bhde->bhce', qd_i, s) + jnp.einsum('bhij,bhje->bhie', at_i, v_new)
        s = s * jnp.exp(gl_i)[..., None, None] + jnp.einsum('bhcd,bhce->bhde', kt_i, v_new)
        return s, o_i

    xs = tuple(jnp.moveaxis(a, 1, 0) for a in (u, w, qd, attn, kt, g_last))
    s_fin, o = lax.scan(step, s0, xs)
    o = jnp.transpose(o, (1, 0, 3, 2, 4)).reshape(bsz, t, h, dv)
    return o.astype(out_dtype), s_fin


def reverse_delta(q, k, v, beta, g, s0):
    f = lambda a: a[:, ::-1]
    o, s = gated_delta_chunked(f(q), f(k), f(v), f(beta), f(g), s0)
    return f(o), s


def mixer_ab(hl, hc, w_in, conv_w, a_log, dt_bias, gnorm, sink, w_out, cos, sin, ctx_out):
    def prep(h, rope):
        bsz, t, _ = h.shape
        aq, ak, av, bqkv, bz, bbf, bbb, baf, bab = _split(h @ w_in, AB_SIZES)
        aq = aq.reshape(bsz, t, A_HEADS, A_HD)
        ak = ak.reshape(bsz, t, A_KV, A_HD)
        av = av.reshape(bsz, t, A_KV, A_HD)
        if rope:
            aq, ak = apply_rope(aq, cos, sin), apply_rope(ak, cos, sin)
        bq, bk, bv = _split(jax.nn.silu(short_conv(bqkv, conv_w)), (B_QK, B_QK, B_VW))
        bq = l2_normalize(bq.reshape(bsz, t, B_HEADS, B_DK)) * B_DK ** -0.5
        bk = l2_normalize(bk.reshape(bsz, t, B_HEADS, B_DK))
        bv = bv.reshape(bsz, t, B_HEADS, B_DV)
        beta = (jax.nn.sigmoid(bbf), jax.nn.sigmoid(bbb))
        gdec = (-jnp.exp(a_log[0]) * jax.nn.softplus(baf.astype(jnp.float32) + dt_bias[0]),
                -jnp.exp(a_log[1]) * jax.nn.softplus(bab.astype(jnp.float32) + dt_bias[1]))
        return (aq, ak, av), (bq, bk, bv, beta, gdec, bz.reshape(bsz, t, B_HEADS, B_DV))

    (aql, akl, avl), (bql, bkl, bvl, betal, gl, zl) = prep(hl, True)
    (aqc, akc, avc), (bqc, bkc, bvc, betac, gcx, zc) = prep(hc, False)
    bsz, t = hl.shape[:2]
    ol_a = window_sink_attention(aql, akl, avl, akc, avc, sink)
    s0 = jnp.zeros((bsz, B_HEADS, B_DK, B_DV), jnp.float32)
    oc_f, sf = gated_delta_chunked(bqc, bkc, bvc, betac[0], gcx[0], s0)
    oc_b, sb = reverse_delta(bqc, bkc, bvc, betac[1], gcx[1], s0)
    ol_f, _ = gated_delta_chunked(bql, bkl, bvl, betal[0], gl[0], sf)
    ol_b, _ = reverse_delta(bql, bkl, bvl, betal[1], gl[1], sb)
    ol_b2 = (rms_norm(ol_f + ol_b, gnorm) * jax.nn.silu(zl)).reshape(bsz, t, B_VW)
    yl = jnp.concatenate([ol_a, ol_b2], axis=-1) @ w_out
    if not ctx_out:
        return yl, None
    lc = hc.shape[1]
    oc_a = ctx_attention(aqc, akc, avc, A_KV, sink)
    oc_b2 = (rms_norm(oc_f + oc_b, gnorm) * jax.nn.silu(zc)).reshape(bsz, lc, B_VW)
    yc = jnp.concatenate([oc_a, oc_b2], axis=-1) @ w_out
    return yl, yc


def full_attention_blocks(q, k, v, kc, vc):
    bsz, t = q.shape[:2]
    nb = t // Q_BLOCK
    g = C_HEADS // C_KV
    k_all = jnp.concatenate([kc, k], axis=1)
    v_all = jnp.concatenate([vc, v], axis=1)
    qb = jnp.moveaxis(q.reshape(bsz, nb, Q_BLOCK, C_KV, g, C_HD), 1, 0)

    def one_block(qblk):
        s = jnp.einsum('bqhgd,bkhd->bhgqk', qblk, k_all).astype(jnp.float32) * C_HD ** -0.5
        p = jax.nn.softmax(s, axis=-1).astype(v_all.dtype)
        return jnp.einsum('bhgqk,bkhd->bqhgd', p, v_all)

    o = lax.map(one_block, qb)
    return jnp.moveaxis(o, 0, 1).reshape(bsz, t, C_HEADS * C_HD)


def mixer_c(hl, hc, w_in, q_norm, k_norm, w_out, cos, sin, ctx_out):
    def prep(h, rope):
        bsz, t, _ = h.shape
        q, k, v = _split(h @ w_in, (C_HEADS * C_HD, C_KV * C_HD, C_KV * C_HD))
        q = rms_norm(q.reshape(bsz, t, C_HEADS, C_HD), q_norm)
        k = rms_norm(k.reshape(bsz, t, C_KV, C_HD), k_norm)
        v = v.reshape(bsz, t, C_KV, C_HD)
        if rope:
            q, k = apply_rope(q, cos, sin), apply_rope(k, cos, sin)
        return q, k, v

    ql, kl, vl = prep(hl, True)
    qc, kc, vc = prep(hc, False)
    yl = full_attention_blocks(ql, kl, vl, kc, vc) @ w_out
    if not ctx_out:
        return yl, None
    return yl, ctx_attention(qc, kc, vc, C_KV) @ w_out


def _mod(m, s):
    return m[:, 3 * s], m[:, 3 * s + 1], m[:, 3 * s + 2]


def ffn_sublayer(x, m, s, w_gu, w_down, g, b):
    shift, scale, gate = _mod(m, s)
    y = swiglu(x * (1 + scale) + shift, w_gu, w_down)
    return layer_norm(DEEP_ALPHA * x + FFN_HALF * gate * y, g, b)


def setup_inputs(seed: int = 0) -> dict:
    key = jax.random.key(seed)
    ks = jax.random.split(key, 24)
    f32 = jnp.float32
    nrm = lambda k, shape, sc: jax.random.normal(k, shape, f32) * sc
    dt = jnp.exp(jax.random.uniform(ks[13], (N_EVEN, 2, B_HEADS), f32, np.log(1e-3), np.log(1e-1)))
    return {
        'x': nrm(ks[0], (BATCH, SEQ, D_MODEL), 1.0),
        'c': nrm(ks[1], (BATCH, D_MODEL), 1.0),
        'ctx': nrm(ks[2], (BATCH, CTX_LEN, D_MODEL), 1.0),
        'c_ctx': nrm(ks[3], (D_MODEL,), 1.0),
        'ada_w': nrm(ks[4], (DEPTH, D_MODEL, N_MOD * D_MODEL), 0.5 * D_MODEL ** -0.5),
        'ada_b': nrm(ks[5], (DEPTH, N_MOD * D_MODEL), 0.02),
        'ln_g': 1.0 + nrm(ks[6], (DEPTH, N_SUB, D_MODEL), 0.02),
        'ln_b': nrm(ks[7], (DEPTH, N_SUB, D_MODEL), 0.02),
        'ffn_w_gu': nrm(ks[8], (DEPTH, 2, D_MODEL, 2 * D_FF), D_MODEL ** -0.5),
        'ffn_w_down': nrm(ks[9], (DEPTH, 2, D_FF, D_MODEL), DEEP_BETA * D_FF ** -0.5),
        'ab_w_in': nrm(ks[10], (N_EVEN, D_MODEL, AB_IN), D_MODEL ** -0.5),
        'ab_conv_w': nrm(ks[11], (N_EVEN, CONV_K, B_QKV), CONV_K ** -0.5),
        'ab_a_log': jnp.log(jax.random.uniform(ks[12], (N_EVEN, 2, B_HEADS), f32, 1.0, 16.0)),
        'ab_dt_bias': dt + jnp.log(-jnp.expm1(-dt)),
        'ab_gnorm': 1.0 + nrm(ks[14], (N_EVEN, B_DV), 0.02),
        'ab_sink': nrm(ks[15], (N_EVEN, A_HEADS), 0.5),
        'ab_w_out': nrm(ks[16], (N_EVEN, AB_OUT, D_MODEL), DEEP_BETA * AB_OUT ** -0.5),
        'c_w_in': nrm(ks[17], (N_ODD, D_MODEL, C_IN), D_MODEL ** -0.5),
        'c_q_norm': 1.0 + nrm(ks[18], (N_ODD, C_HD), 0.02),
        'c_k_norm': 1.0 + nrm(ks[19], (N_ODD, C_HD), 0.02),
        'c_w_out': nrm(ks[20], (N_ODD, C_OUT, D_MODEL), DEEP_BETA * C_OUT ** -0.5),
    }


def reference(x, c, ctx, c_ctx, ada_w, ada_b, ln_g, ln_b, ffn_w_gu, ffn_w_down, ab_w_in, ab_conv_w, ab_a_log,
              ab_dt_bias, ab_gnorm, ab_sink, ab_w_out, c_w_in, c_q_norm, c_k_norm, c_w_out):
    bsz, t, _ = x.shape
    rows = t // GRID_W
    cos_a, sin_a = axial_rope(rows, A_HD)
    cos_c, sin_c = axial_rope(rows, C_HD)
    xl, xc = x, ctx
    for l in range(DEPTH):
        ctx_out = l < DEPTH - 1
        m_l = (jax.nn.silu(c) @ ada_w[l] + ada_b[l]).reshape(bsz, N_MOD, 1, D_MODEL)
        m_c = (jax.nn.silu(c_ctx) @ ada_w[l] + ada_b[l]).reshape(1, N_MOD, 1, D_MODEL)
        xl = ffn_sublayer(xl, m_l, 0, ffn_w_gu[l, 0], ffn_w_down[l, 0], ln_g[l, 0], ln_b[l, 0])
        xc = ffn_sublayer(xc, m_c, 0, ffn_w_gu[l, 0], ffn_w_down[l, 0], ln_g[l, 0], ln_b[l, 0])
        sh_l, sc_l, gt_l = _mod(m_l, 1)
        sh_c, sc_c, gt_c = _mod(m_c, 1)
        hl = xl * (1 + sc_l) + sh_l
        hc = xc * (1 + sc_c) + sh_c
        i = l // 2
        if l % 2 == 0:
            yl, yc = mixer_ab(hl, hc, ab_w_in[i], ab_conv_w[i], ab_a_log[i], ab_dt_bias[i], ab_gnorm[i], ab_sink[i],
                              ab_w_out[i], cos_a, sin_a, ctx_out)
        else:
            yl, yc = mixer_c(hl, hc, c_w_in[i], c_q_norm[i], c_k_norm[i], c_w_out[i], cos_c, sin_c, ctx_out)
        xl = layer_norm(DEEP_ALPHA * xl + gt_l * yl, ln_g[l, 1], ln_b[l, 1])
        xl = ffn_sublayer(xl, m_l, 2, ffn_w_gu[l, 1], ffn_w_down[l, 1], ln_g[l, 2], ln_b[l, 2])
        if ctx_out:
            xc = layer_norm(DEEP_ALPHA * xc + gt_c * yc, ln_g[l, 1], ln_b[l, 1])
            xc = ffn_sublayer(xc, m_c, 2, ffn_w_gu[l, 1], ffn_w_down[l, 1], ln_g[l, 2], ln_b[l, 2])
    return xl
```

```python
import functools

import jax
import jax.numpy as jnp
from jax import lax
from jax.experimental import pallas as pl
from jax.experimental.pallas import tpu as pltpu

F32 = jnp.float32
BF16 = jnp.bfloat16

D_MODEL = 1024
DEPTH = 4
GRID_W = 64
D_FF = 2816
N_SUB = 3
N_MOD = 3 * N_SUB
FFN_HALF = 0.5
NORM_EPS = 1e-6
ROPE_THETA = 10000.0
DEEP_ALPHA = (2 * DEPTH) ** 0.25

A_HEADS, A_KV, A_HD = 8, 2, 64
WINDOW = 128
BLOCK = 128
B_HEADS, B_DK, B_DV = 4, 128, 128
CONV_K = 5
CHUNK = 64
C_HEADS, C_KV, C_HD = 8, 2, 128

A_Q = A_HEADS * A_HD
A_KVW = A_KV * A_HD
B_QK = B_HEADS * B_DK
B_VW = B_HEADS * B_DV
B_QKV = 2 * B_QK + B_VW
AB_IN = A_Q + 2 * A_KVW + B_QKV + B_VW + 4 * B_HEADS
N_GATE = 4 * B_HEADS
LANE = 128
AB_IN_PAD = AB_IN - N_GATE + LANE
OFF_AK = A_Q
OFF_AV = A_Q + A_KVW
OFF_BQKV = A_Q + 2 * A_KVW
OFF_BZ = OFF_BQKV + B_QKV
OFF_GATE = OFF_BZ + B_VW
C_QW = C_HEADS * C_HD
C_KW = C_KV * C_HD
C_IN = C_QW + 2 * C_KW
HALO = 8
NEG = -1e30
MIB = 1024 * 1024


def _cparams(sem, vmem_mib):
    return pltpu.CompilerParams(dimension_semantics=sem, vmem_limit_bytes=vmem_mib * MIB)


def _dot(a, b):
    return jnp.dot(a, b, preferred_element_type=F32)


def _dot_nt(a, b):
    return lax.dot_general(a, b, (((1,), (1,)), ((), ())), preferred_element_type=F32)


def _silu(x):
    return x * jax.nn.sigmoid(x)


def _res_ln(x, y, gate, g, b):
    z = DEEP_ALPHA * x + gate * y
    mu = jnp.mean(z, axis=-1, keepdims=True)
    zc = z - mu
    var = jnp.mean(zc * zc, axis=-1, keepdims=True)
    return zc * lax.rsqrt(var + NORM_EPS) * g + b


def _token_tile(t):
    return 512 if t % 512 == 0 else 256


def _mod_body(c_ref, w_ref, b_ref, o_ref):
    c = c_ref[...]
    o_ref[...] = jnp.dot(_silu(c), w_ref[...], preferred_element_type=F32,
                         precision=lax.Precision.HIGHEST) + b_ref[...]


def _modulation(cvec, ada_w, ada_b):
    depth, d, n = ada_w.shape
    rows = cvec.shape[0]
    tn = 2304
    return pl.pallas_call(
        _mod_body,
        grid=(depth, n // tn),
        in_specs=[pl.BlockSpec((rows, d), lambda l, j: (0, 0)),
                  pl.BlockSpec((None, d, tn), lambda l, j: (l, 0, j)),
                  pl.BlockSpec((None, 1, tn), lambda l, j: (l, 0, j))],
        out_specs=pl.BlockSpec((None, rows, tn), lambda l, j: (l, 0, j)),
        out_shape=jax.ShapeDtypeStruct((depth, rows, n), F32),
        compiler_params=_cparams(("parallel", "parallel"), 40),
        name="mod",
    )(cvec, ada_w, ada_b.reshape(depth, 1, n))


FFN_CHUNKS = 2


def _ffn_body(x_ref, mod_ref, wgu_ref, wd_ref, o_ref):
    x = x_ref[...]
    mod = mod_ref[...]
    h = (x * (1.0 + mod[1:2]) + mod[0:1]).astype(BF16)
    fc = D_FF // FFN_CHUNKS
    y = None
    for j in range(FFN_CHUNKS):
        gt = _dot(h, wgu_ref[:, j * fc:(j + 1) * fc])
        up = _dot(h, wgu_ref[:, D_FF + j * fc:D_FF + (j + 1) * fc])
        act = (_silu(gt) * up).astype(BF16)
        part = _dot(act, wd_ref[j * fc:(j + 1) * fc, :])
        y = part if y is None else y + part
    o_ref[...] = _res_ln(x, y, FFN_HALF * mod[2:3], mod[3:4], mod[4:5])


def _ffn(x, mod, wgu, wd, l, s):
    bsz, t, d = x.shape
    tm = _token_tile(t)
    per_batch = mod.shape[0] > 1
    return pl.pallas_call(
        _ffn_body,
        grid=(bsz, t // tm),
        in_specs=[pl.BlockSpec((None, tm, d), lambda b, i: (b, i, 0)),
                  pl.BlockSpec((None, 8, d), lambda b, i: (b if per_batch else 0, 0, 0)),
                  pl.BlockSpec((None, None, d, 2 * D_FF), lambda b, i: (l, s, 0, 0),
                               pipeline_mode=pl.Buffered(1)),
                  pl.BlockSpec((None, None, D_FF, d), lambda b, i: (l, s, 0, 0),
                               pipeline_mode=pl.Buffered(1))],
        out_specs=pl.BlockSpec((None, tm, d), lambda b, i: (b, i, 0)),
        out_shape=jax.ShapeDtypeStruct(x.shape, F32),
        compiler_params=_cparams(("parallel", "parallel"), 52),
        name="ffn",
    )(x, mod, wgu, wd)


def _rope_full(x, cos2, sin2):
    return x * cos2 + pltpu.roll(x, C_HD // 2, 1) * sin2


def _c_prep_body(*refs, rope):
    if rope:
        x_ref, mod_ref, w_ref, qn_ref, kn_ref, cos_ref, sin_ref, q_ref, k_ref, v_ref = refs
        cos2, sin2 = cos_ref[...], sin_ref[...]
    else:
        x_ref, mod_ref, w_ref, qn_ref, kn_ref, q_ref, k_ref, v_ref = refs
    x = x_ref[...]
    mod = mod_ref[...]
    h = (x * (1.0 + mod[1:2]) + mod[0:1]).astype(BF16)
    qkv = _dot(h, w_ref[...])
    qn = qn_ref[...] * (C_HD ** -0.5)
    kn = kn_ref[...]

    def head(col, gain):
        a = qkv[:, col:col + C_HD]
        a = a * lax.rsqrt(jnp.mean(a * a, axis=-1, keepdims=True) + NORM_EPS) * gain
        if rope:
            a = _rope_full(a, cos2, sin2)
        return a.astype(BF16)

    for j in range(C_HEADS):
        q_ref[:, j * C_HD:(j + 1) * C_HD] = head(j * C_HD, qn)
    for j in range(C_KV):
        k_ref[:, j * C_HD:(j + 1) * C_HD] = head(C_QW + j * C_HD, kn)
    v_ref[...] = qkv[:, C_QW + C_KW:].astype(BF16)


def _c_prep(x, mod, w_in, l, qn, kn, tables):
    bsz, t, d = x.shape
    tm = _token_tile(t)
    per_batch = mod.shape[0] > 1
    rope = tables is not None
    in_specs = [pl.BlockSpec((None, tm, d), lambda b, i: (b, i, 0)),
                pl.BlockSpec((None, 8, d), lambda b, i: (b if per_batch else 0, 0, 0)),
                pl.BlockSpec((None, d, C_IN), lambda b, i: (l, 0, 0), pipeline_mode=pl.Buffered(1)),
                pl.BlockSpec((1, C_HD), lambda b, i: (0, 0)),
                pl.BlockSpec((1, C_HD), lambda b, i: (0, 0))]
    args = [x, mod, w_in, qn, kn]
    if rope:
        in_specs += [pl.BlockSpec((tm, C_HD), lambda b, i: (i, 0))] * 2
        args += list(tables)
    return pl.pallas_call(
        functools.partial(_c_prep_body, rope=rope),
        grid=(bsz, t // tm),
        in_specs=in_specs,
        out_specs=[pl.BlockSpec((None, tm, C_QW), lambda b, i: (b, i, 0)),
                   pl.BlockSpec((None, tm, C_KW), lambda b, i: (b, i, 0)),
                   pl.BlockSpec((None, tm, C_KW), lambda b, i: (b, i, 0))],
        out_shape=[jax.ShapeDtypeStruct((bsz, t, C_QW), BF16),
                   jax.ShapeDtypeStruct((bsz, t, C_KW), BF16),
                   jax.ShapeDtypeStruct((bsz, t, C_KW), BF16)],
        compiler_params=_cparams(("parallel", "parallel"), 40),
        name="c_prep",
    )(*args)


C_GROUP = C_HEADS // C_KV
TQ_C = 128


def _flash_body(*refs, latent):
    if latent:
        q_ref, kc_ref, vc_ref, kl_ref, vl_ref, o_ref = refs
    else:
        q_ref, kc_ref, vc_ref, o_ref = refs
    qs = jnp.concatenate([q_ref[:, g * C_HD:(g + 1) * C_HD] for g in range(C_GROUP)], axis=0)
    s_c = _dot_nt(qs, kc_ref[...])
    m = jnp.max(s_c, axis=-1, keepdims=True)
    if latent:
        s_l = _dot_nt(qs, kl_ref[...])
        m = jnp.maximum(m, jnp.max(s_l, axis=-1, keepdims=True))
    p_c = jnp.exp(s_c - m)
    den = jnp.sum(p_c, axis=-1, keepdims=True)
    o = _dot(p_c.astype(BF16), vc_ref[...])
    if latent:
        p_l = jnp.exp(s_l - m)
        den = den + jnp.sum(p_l, axis=-1, keepdims=True)
        o = o + _dot(p_l.astype(BF16), vl_ref[...])
    o = o / den
    tq = q_ref.shape[0]
    for g in range(C_GROUP):
        o_ref[:, g * C_HD:(g + 1) * C_HD] = o[g * tq:(g + 1) * tq].astype(BF16)


def _flash(q, kc, vc, kl=None, vl=None):
    bsz, t, _ = q.shape
    lc = kc.shape[1]
    latent = kl is not None
    tq = TQ_C
    gw = C_GROUP * C_HD
    in_specs = [pl.BlockSpec((None, tq, gw), lambda b, h, i: (b, i, h)),
                pl.BlockSpec((None, lc, C_HD), lambda b, h, i: (b, 0, h)),
                pl.BlockSpec((None, lc, C_HD), lambda b, h, i: (b, 0, h))]
    args = [q, kc, vc]
    if latent:
        in_specs += [pl.BlockSpec((None, t, C_HD), lambda b, h, i: (b, 0, h))] * 2
        args += [kl, vl]
    return pl.pallas_call(
        functools.partial(_flash_body, latent=latent),
        grid=(bsz, C_KV, t // tq),
        in_specs=in_specs,
        out_specs=pl.BlockSpec((None, tq, gw), lambda b, h, i: (b, i, h)),
        out_shape=jax.ShapeDtypeStruct((bsz, t, C_QW), BF16),
        compiler_params=_cparams(("parallel", "parallel", "parallel"), 48),
        name="flash",
    )(*args)


def _c_out_body(x_ref, mod_ref, o_ref, w_ref, out_ref):
    mod = mod_ref[...]
    y = _dot(o_ref[...], w_ref[...])
    out_ref[...] = _res_ln(x_ref[...], y, mod[2:3], mod[3:4], mod[4:5])


def _c_out(x, mod, o, w_out, l):
    bsz, t, d = x.shape
    tm = _token_tile(t)
    per_batch = mod.shape[0] > 1
    return pl.pallas_call(
        _c_out_body,
        grid=(bsz, t // tm),
        in_specs=[pl.BlockSpec((None, tm, d), lambda b, i: (b, i, 0)),
                  pl.BlockSpec((None, 8, d), lambda b, i: (b if per_batch else 0, 0, 0)),
                  pl.BlockSpec((None, tm, C_QW), lambda b, i: (b, i, 0)),
                  pl.BlockSpec((None, C_QW, d), lambda b, i: (l, 0, 0), pipeline_mode=pl.Buffered(1))],
        out_specs=pl.BlockSpec((None, tm, d), lambda b, i: (b, i, 0)),
        out_shape=jax.ShapeDtypeStruct(x.shape, F32),
        compiler_params=_cparams(("parallel", "parallel"), 32),
        name="c_out",
    )(x, mod, o, w_out)


def _rope_half_tile(x, cos2, sin2, lane_lo):
    half = A_HD // 2
    partner = jnp.where(lane_lo, pltpu.roll(x, LANE - half, 1), pltpu.roll(x, half, 1))
    return x * cos2 + partner * sin2


def _gate_act(raw, is_beta, a_log, dt_bias):
    sp_in = raw + dt_bias
    softplus = jnp.maximum(sp_in, 0.0) + jnp.log(1.0 + jnp.exp(-jnp.abs(sp_in)))
    return jnp.where(is_beta, jax.nn.sigmoid(raw), -jnp.exp(a_log) * softplus)


def _ab_prep_body(*refs, rope, tm):
    if rope:
        (xp_ref, x_ref, xn_ref, mod_ref, w_ref, wgt_ref, cw_ref, gp_ref, gpt_ref, cos_ref, sin_ref,
         aq_ref, ak_ref, av_ref, bq_ref, bk_ref, bv_ref, z_ref, gc_ref, gr_ref, cbuf) = refs
    else:
        (xp_ref, x_ref, xn_ref, mod_ref, w_ref, wgt_ref, cw_ref, gp_ref, gpt_ref,
         aq_ref, ak_ref, av_ref, bq_ref, bk_ref, bv_ref, z_ref, gc_ref, gr_ref, cbuf) = refs
    i = pl.program_id(1)
    nt = pl.num_programs(1)
    mod = mod_ref[...]
    scale1, shift = 1.0 + mod[1:2], mod[0:1]
    h = (x_ref[...] * scale1 + shift).astype(BF16)
    proj = _dot(h, w_ref[...])

    if rope:
        cos2, sin2 = cos_ref[...], sin_ref[...]
        lane_lo = (lax.broadcasted_iota(jnp.int32, (tm, LANE), 1) % A_HD) < (A_HD // 2)
    for j in range(A_Q // LANE):
        a = proj[:, j * LANE:(j + 1) * LANE]
        if rope:
            a = _rope_half_tile(a, cos2, sin2, lane_lo)
        aq_ref[:, j * LANE:(j + 1) * LANE] = (a * (A_HD ** -0.5)).astype(BF16)
    a = proj[:, OFF_AK:OFF_AK + LANE]
    if rope:
        a = _rope_half_tile(a, cos2, sin2, lane_lo)
    ak_ref[...] = a.astype(BF16)
    av_ref[...] = proj[:, OFF_AV:OFF_AV + LANE].astype(BF16)
    z_ref[...] = proj[:, OFF_BZ:OFF_BZ + B_VW]

    gp = gp_ref[...]
    lane = lax.broadcasted_iota(jnp.int32, (1, LANE), 1)
    gc_ref[...] = _gate_act(proj[:, OFF_GATE:OFF_GATE + LANE], lane < 2 * B_HEADS, gp[0:1], gp[1:2])
    gpt = gpt_ref[...]
    row = lax.broadcasted_iota(jnp.int32, (N_GATE, 1), 0)
    gr_ref[...] = _gate_act(_dot_nt(wgt_ref[...], h), row < 2 * B_HEADS, gpt[:, 0:1], gpt[:, 1:2])

    w_b = w_ref[:, OFF_BQKV:OFF_BQKV + B_QKV]
    hp = (xp_ref[...] * scale1 + shift).astype(BF16)
    hn = (xn_ref[...] * scale1 + shift).astype(BF16)
    cbuf[0:HALO, :] = _dot(hp, w_b) * jnp.where(i > 0, 1.0, 0.0)
    cbuf[HALO:HALO + tm, :] = proj[:, OFF_BQKV:OFF_BQKV + B_QKV]
    cbuf[HALO + tm:, :] = _dot(hn, w_b) * jnp.where(i < nt - 1, 1.0, 0.0)
    cw = cw_ref[...]
    pad = CONV_K // 2
    for part, dst in enumerate((bq_ref, bk_ref, bv_ref)):
        for hd in range(B_HEADS):
            c0 = part * B_QK + hd * B_DK
            acc = None
            for j in range(CONV_K):
                term = cbuf[HALO - pad + j:HALO - pad + j + tm, c0:c0 + B_DK] * cw[j:j + 1, c0:c0 + B_DK]
                acc = term if acc is None else acc + term
            acc = _silu(acc)
            if part < 2:
                acc = acc * lax.rsqrt(jnp.sum(acc * acc, axis=-1, keepdims=True) + NORM_EPS)
            if part == 0:
                acc = acc * (B_DK ** -0.5)
            dst[:, hd * B_DK:(hd + 1) * B_DK] = acc


def _ab_prep(x, mod, w_pad, w_gate_t, conv_w, gate_p, gate_pt, l, tables):
    bsz, t, d = x.shape
    tm = _token_tile(t)
    nh = tm // HALO
    nblk = t // HALO
    per_batch = mod.shape[0] > 1
    rope = tables is not None
    const = dict(pipeline_mode=pl.Buffered(1))
    in_specs = [pl.BlockSpec((None, HALO, d), lambda b, i: (b, jnp.maximum(i * nh - 1, 0), 0)),
                pl.BlockSpec((None, tm, d), lambda b, i: (b, i, 0)),
                pl.BlockSpec((None, HALO, d), lambda b, i: (b, jnp.minimum((i + 1) * nh, nblk - 1), 0)),
                pl.BlockSpec((None, 8, d), lambda b, i: (b if per_batch else 0, 0, 0)),
                pl.BlockSpec((None, d, AB_IN_PAD), lambda b, i: (l, 0, 0), **const),
                pl.BlockSpec((None, N_GATE, d), lambda b, i: (l, 0, 0), **const),
                pl.BlockSpec((None, 8, B_QKV), lambda b, i: (l, 0, 0)),
                pl.BlockSpec((None, 8, LANE), lambda b, i: (l, 0, 0)),
                pl.BlockSpec((None, N_GATE, LANE), lambda b, i: (l, 0, 0))]
    args = [x, x, x, mod, w_pad, w_gate_t, conv_w, gate_p, gate_pt]
    if rope:
        in_specs += [pl.BlockSpec((tm, LANE), lambda b, i: (i, 0))] * 2
        args += list(tables)
    tok = lambda w: pl.BlockSpec((None, tm, w), lambda b, i: (b, i, 0))
    shp = lambda w, dt: jax.ShapeDtypeStruct((bsz, t, w), dt)
    return pl.pallas_call(
        functools.partial(_ab_prep_body, rope=rope, tm=tm),
        grid=(bsz, t // tm),
        in_specs=in_specs,
        out_specs=[tok(A_Q), tok(A_KVW), tok(A_KVW), tok(B_QK), tok(B_QK), tok(B_VW), tok(B_VW), tok(LANE),
                   pl.BlockSpec((None, N_GATE, tm), lambda b, i: (b, 0, i))],
        out_shape=[shp(A_Q, BF16), shp(A_KVW, BF16), shp(A_KVW, BF16), shp(B_QK, F32), shp(B_QK, F32),
                   shp(B_VW, F32), shp(B_VW, F32), shp(LANE, F32),
                   jax.ShapeDtypeStruct((bsz, N_GATE, t), F32)],
        scratch_shapes=[pltpu.VMEM((tm + 2 * HALO, B_QKV), F32)],
        compiler_params=_cparams(("parallel", "parallel"), 52),
        name="ab_prep",
    )(*args)


A_GROUP = A_HEADS // A_KV


def _win_body(*refs, local):
    if local:
        q_ref, kp_ref, kc_ref, kn_ref, vp_ref, vc_ref, vn_ref, kx_ref, vx_ref, sink_ref, o_ref = refs
    else:
        q_ref, kx_ref, vx_ref, sink_ref, o_ref = refs
    tq = q_ref.shape[0]
    rows = A_GROUP * tq
    if local:
        i = pl.program_id(1)
        nb = pl.num_programs(1)
        r = lax.broadcasted_iota(jnp.int32, (rows, 3 * BLOCK), 0) % tq
        c = lax.broadcasted_iota(jnp.int32, (rows, 3 * BLOCK), 1)
        valid = jnp.abs(c - r - BLOCK) <= WINDOW
        valid = valid & ((c >= BLOCK) | (i > 0)) & ((c < 2 * BLOCK) | (i < nb - 1))
        k3 = jnp.concatenate([kp_ref[...], kc_ref[...], kn_ref[...]], axis=0)
        v3 = jnp.concatenate([vp_ref[...], vc_ref[...], vn_ref[...]], axis=0)
    kx = kx_ref[...]
    vx = vx_ref[...]
    sink = sink_ref[...]
    for kv in range(A_KV):
        qs = jnp.concatenate(
            [q_ref[:, (kv * A_GROUP + g) * A_HD:(kv * A_GROUP + g + 1) * A_HD] for g in range(A_GROUP)], axis=0)
        sk = jnp.concatenate(
            [jnp.broadcast_to(sink[kv * A_GROUP + g:kv * A_GROUP + g + 1, 0:1], (tq, 1)) for g in range(A_GROUP)],
            axis=0)
        lo, hi = kv * A_HD, (kv + 1) * A_HD
        s_x = _dot_nt(qs, kx[:, lo:hi])
        m = jnp.maximum(jnp.max(s_x, axis=-1, keepdims=True), sk)
        if local:
            s_l = jnp.where(valid, _dot_nt(qs, k3[:, lo:hi]), NEG)
            m = jnp.maximum(m, jnp.max(s_l, axis=-1, keepdims=True))
        p_x = jnp.exp(s_x - m)
        den = jnp.sum(p_x, axis=-1, keepdims=True) + jnp.exp(sk - m)
        o = _dot(p_x.astype(BF16), vx[:, lo:hi])
        if local:
            p_l = jnp.exp(s_l - m)
            den = den + jnp.sum(p_l, axis=-1, keepdims=True)
            o = o + _dot(p_l.astype(BF16), v3[:, lo:hi])
        o = o / den
        for g in range(A_GROUP):
            hq = kv * A_GROUP + g
            o_ref[:, hq * A_HD:(hq + 1) * A_HD] = o[g * tq:(g + 1) * tq].astype(BF16)


def _win_attn(q, kx, vx, sink, l, k=None, v=None):
    bsz, t, _ = q.shape
    lc = kx.shape[1]
    local = k is not None
    nb = t // BLOCK
    blk = lambda f: pl.BlockSpec((None, BLOCK, A_KVW), f)
    in_specs = [pl.BlockSpec((None, BLOCK, A_Q), lambda b, i: (b, i, 0))]
    args = [q]
    if local:
        prv = lambda b, i: (b, jnp.maximum(i - 1, 0), 0)
        cur = lambda b, i: (b, i, 0)
        nxt = lambda b, i: (b, jnp.minimum(i + 1, nb - 1), 0)
        in_specs += [blk(prv), blk(cur), blk(nxt), blk(prv), blk(cur), blk(nxt)]
        args += [k, k, k, v, v, v]
    in_specs += [pl.BlockSpec((None, lc, A_KVW), lambda b, i: (b, 0, 0)),
                 pl.BlockSpec((None, lc, A_KVW), lambda b, i: (b, 0, 0)),
                 pl.BlockSpec((None, A_HEADS, LANE), lambda b, i: (l, 0, 0))]
    args += [kx, vx, sink]
    return pl.pallas_call(
        functools.partial(_win_body, local=local),
        grid=(bsz, nb),
        in_specs=in_specs,
        out_specs=pl.BlockSpec((None, BLOCK, A_Q), lambda b, i: (b, i, 0)),
        out_shape=jax.ShapeDtypeStruct((bsz, t, A_Q), BF16),
        compiler_params=_cparams(("parallel", "parallel"), 32),
        name="win_attn",
    )(*args)


DELTA_BLOCK = 2 * CHUNK
N_CHAIN = 2 * B_HEADS


def _split_bf16(a):
    hi = a.astype(BF16)
    return hi, (a - hi.astype(F32)).astype(BF16)


def _delta_chunk(q, k, v, beta, gcc, gcr, gtot, s, incl, strict, eye):
    decay = jnp.exp(jnp.where(incl, gcc - gcr, NEG))
    kb = k * beta
    a = _dot_nt(jnp.concatenate([kb, q], axis=0).astype(BF16), k.astype(BF16))
    p = jnp.where(strict, a[:CHUNK] * decay, 0.0) * -1.0
    attn = a[CHUNK:] * decay
    t = eye + p
    n_sq = CHUNK.bit_length() - 2
    for _ in range(n_sq):
        pb = p.astype(BF16)
        p = _dot(pb, pb)
        t = t + _dot(t.astype(BF16), p.astype(BF16))
    eg = jnp.exp(gcc)
    rhs = jnp.concatenate([v * beta, kb * eg], axis=1).astype(BF16)
    uw = _dot(t.astype(BF16), rhs)
    u, w = uw[:, :B_DV], uw[:, B_DV:]
    qd = q * eg
    kt = k * jnp.exp(gtot - gcc)
    ws = _dot(jnp.concatenate([w, qd], axis=0).astype(BF16), s.astype(BF16))
    v_new = u - ws[:CHUNK]
    v_new_b = v_new.astype(BF16)
    o = ws[CHUNK:] + _dot(attn.astype(BF16), v_new_b)
    s_new = s * jnp.exp(gtot) + _dot(kt.T.astype(BF16), v_new_b)
    return o, s_new


def _delta_body(qf_ref, kf_ref, vf_ref, gcf_ref, grf_ref, qb_ref, kb_ref, vb_ref, gcb_ref, grb_ref, s0_ref,
                of_ref, ob_ref, s_ref):
    @pl.when(pl.program_id(1) == 0)
    def _():
        s_ref[...] = s0_ref[...]

    li = lax.broadcasted_iota(jnp.int32, (CHUNK, CHUNK), 0)
    lj = lax.broadcasted_iota(jnp.int32, (CHUNK, CHUNK), 1)
    eye = jnp.where(li == lj, 1.0, 0.0)
    tri_l = jnp.where(li >= lj, 1.0, 0.0).astype(BF16)
    tri_u = jnp.where(li <= lj, 1.0, 0.0).astype(BF16)

    dirs = ((qf_ref, kf_ref, vf_ref, gcf_ref, grf_ref, of_ref, (0, 1), tri_l, tri_u, li >= lj, li > lj, CHUNK - 1),
            (qb_ref, kb_ref, vb_ref, gcb_ref, grb_ref, ob_ref, (1, 0), tri_u, tri_l, li <= lj, li < lj, 0))
    for d, (q_ref, k_ref, v_ref, gc_ref, gr_ref, o_ref, order, tri_c, tri_r, incl, strict, last) in enumerate(dirs):
        for c in order:
            r0 = c * CHUNK
            gcol = gc_ref[r0:r0 + CHUNK, :]
            ghi, glo = _split_bf16(gcol)
            cum_c = _dot(tri_c, ghi) + _dot(tri_c, glo)
            grow = gr_ref[:, r0:r0 + CHUNK]
            rhi, rlo = _split_bf16(grow)
            cum_r = _dot(rhi, tri_r) + _dot(rlo, tri_r)
            for hd in range(B_HEADS):
                col = d * B_HEADS + hd
                gl = 2 * B_HEADS + col
                lo, hi = hd * B_DK, (hd + 1) * B_DK
                chain = d * B_HEADS + hd
                o, s_new = _delta_chunk(
                    q_ref[r0:r0 + CHUNK, lo:hi], k_ref[r0:r0 + CHUNK, lo:hi], v_ref[r0:r0 + CHUNK, lo:hi],
                    gcol[:, col:col + 1], cum_c[:, gl:gl + 1], cum_r[gl:gl + 1, :], cum_c[last:last + 1, gl:gl + 1],
                    s_ref[chain], incl, strict, eye)
                o_ref[r0:r0 + CHUNK, lo:hi] = o
                s_ref[chain] = s_new


def _delta(bq, bk, bv, gcol, grow, s0):
    bsz, t, _ = bq.shape
    ns = t // DELTA_BLOCK
    fwd = lambda w: pl.BlockSpec((None, DELTA_BLOCK, w), lambda b, s: (b, s, 0))
    bwd = lambda w: pl.BlockSpec((None, DELTA_BLOCK, w), lambda b, s: (b, ns - 1 - s, 0))
    st = pl.BlockSpec((None, N_CHAIN, B_DK, B_DV), lambda b, s: (b, 0, 0, 0))
    return pl.pallas_call(
        _delta_body,
        grid=(bsz, ns),
        in_specs=[fwd(B_QK), fwd(B_QK), fwd(B_VW), fwd(LANE),
                  pl.BlockSpec((None, N_GATE, DELTA_BLOCK), lambda b, s: (b, 0, s)),
                  bwd(B_QK), bwd(B_QK), bwd(B_VW), bwd(LANE),
                  pl.BlockSpec((None, N_GATE, DELTA_BLOCK), lambda b, s: (b, 0, ns - 1 - s)),
                  st],
        out_specs=[fwd(B_VW), bwd(B_VW), st],
        out_shape=[jax.ShapeDtypeStruct((bsz, t, B_VW), F32), jax.ShapeDtypeStruct((bsz, t, B_VW), F32),
                   jax.ShapeDtypeStruct((bsz, N_CHAIN, B_DK, B_DV), F32)],
        compiler_params=_cparams(("parallel", "arbitrary"), 32),
        name="delta",
    )(bq, bk, bv, gcol, grow, bq, bk, bv, gcol, grow, s0)


def _ab_out_body(x_ref, mod_ref, oa_ref, of_ref, ob_ref, z_ref, gn_ref, w_ref, out_ref):
    mod = mod_ref[...]
    gn = gn_ref[...]
    y = _dot(oa_ref[...], w_ref[0:A_Q, :])
    for hd in range(B_HEADS):
        lo, hi = hd * B_DV, (hd + 1) * B_DV
        o = of_ref[:, lo:hi] + ob_ref[:, lo:hi]
        o = o * lax.rsqrt(jnp.mean(o * o, axis=-1, keepdims=True) + NORM_EPS) * gn
        o = (o * _silu(z_ref[:, lo:hi])).astype(BF16)
        y = y + _dot(o, w_ref[A_Q + lo:A_Q + hi, :])
    out_ref[...] = _res_ln(x_ref[...], y, mod[2:3], mod[3:4], mod[4:5])


def _ab_out(x, mod, oa, of, ob, z, gnorm, w_out, l):
    bsz, t, d = x.shape
    tm = _token_tile(t)
    per_batch = mod.shape[0] > 1
    tok = lambda w: pl.BlockSpec((None, tm, w), lambda b, i: (b, i, 0))
    return pl.pallas_call(
        _ab_out_body,
        grid=(bsz, t // tm),
        in_specs=[tok(d),
                  pl.BlockSpec((None, 8, d), lambda b, i: (b if per_batch else 0, 0, 0)),
                  tok(A_Q), tok(B_VW), tok(B_VW), tok(B_VW),
                  pl.BlockSpec((None, 1, B_DV), lambda b, i: (l, 0, 0)),
                  pl.BlockSpec((None, A_Q + B_VW, d), lambda b, i: (l, 0, 0), pipeline_mode=pl.Buffered(1))],
        out_specs=tok(d),
        out_shape=jax.ShapeDtypeStruct(x.shape, F32),
        compiler_params=_cparams(("parallel", "parallel"), 40),
        name="ab_out",
    )(x, mod, oa, of, ob, z, gnorm, w_out)


def _rope_tables(rows, head_dim):
    n_freq = head_dim // 4
    inv = ROPE_THETA ** (-jnp.arange(n_freq, dtype=F32) / n_freq)
    r, col = jnp.meshgrid(jnp.arange(rows, dtype=F32), jnp.arange(GRID_W, dtype=F32), indexing='ij')
    r, col = r.reshape(-1), col.reshape(-1)
    ang = jnp.concatenate([r[:, None] * inv, col[:, None] * inv], axis=-1)
    cos, sin = jnp.cos(ang), jnp.sin(ang)
    cos2 = jnp.concatenate([cos, cos], axis=-1)
    sin2 = jnp.concatenate([-sin, sin], axis=-1)
    rep = LANE // head_dim
    return jnp.tile(cos2, (1, rep)), jnp.tile(sin2, (1, rep))


def _pad_rows(a, rows):
    return jnp.pad(a, ((0, 0), (0, rows - a.shape[1]), (0, 0)))


def kernel(x, c, ctx, c_ctx, ada_w, ada_b, ln_g, ln_b, ffn_w_gu, ffn_w_down, ab_w_in, ab_conv_w, ab_a_log,
           ab_dt_bias, ab_gnorm, ab_sink, ab_w_out, c_w_in, c_q_norm, c_k_norm, c_w_out):
    bsz, t, d = x.shape
    depth = ada_w.shape[0]
    n_even = ab_w_in.shape[0]
    rows = t // GRID_W
    tab_a = _rope_tables(rows, A_HD)
    tab_c = _rope_tables(rows, C_HD)

    wgu = ffn_w_gu.astype(BF16)
    wd = ffn_w_down.astype(BF16)
    ab_w_pad = jnp.pad(ab_w_in, ((0, 0), (0, 0), (0, AB_IN_PAD - AB_IN))).astype(BF16)
    ab_w_gate_t = jnp.swapaxes(ab_w_in[:, :, OFF_GATE:OFF_GATE + N_GATE], 1, 2).astype(BF16)
    ab_wo = ab_w_out.astype(BF16)
    c_wi = c_w_in.astype(BF16)
    c_wo = c_w_out.astype(BF16)
    conv_w = _pad_rows(ab_conv_w, 8)
    zeros8 = jnp.zeros((n_even, 2 * B_HEADS), F32)
    a_log16 = jnp.concatenate([zeros8, ab_a_log.reshape(n_even, 2 * B_HEADS)], axis=1)
    dtb16 = jnp.concatenate([zeros8, ab_dt_bias.reshape(n_even, 2 * B_HEADS)], axis=1)
    gate_p = _pad_rows(jnp.pad(jnp.stack([a_log16, dtb16], axis=1), ((0, 0), (0, 0), (0, LANE - N_GATE))), 8)
    gate_pt = jnp.pad(jnp.stack([a_log16, dtb16], axis=2), ((0, 0), (0, 0), (0, LANE - 2)))
    sink = jnp.broadcast_to(ab_sink[:, :, None], (n_even, A_HEADS, LANE))
    gnorm = ab_gnorm.reshape(n_even, 1, B_DV)
    qn = c_q_norm.reshape(-1, 1, C_HD)
    kn = c_k_norm.reshape(-1, 1, C_HD)

    nrow = ((bsz + 1 + 7) // 8) * 8
    cvec = jnp.pad(jnp.concatenate([c, c_ctx[None]], axis=0), ((0, nrow - bsz - 1), (0, 0)))
    m = _modulation(cvec, ada_w, ada_b).reshape(depth, nrow, N_MOD, d)

    def mod_rows(l, s):
        ln = jnp.stack([ln_g[l, s], ln_b[l, s]], axis=0)
        rows_l = jnp.concatenate([m[l, :bsz, 3 * s:3 * s + 3], jnp.broadcast_to(ln, (bsz, 2, d)),
                                  jnp.zeros((bsz, 3, d), F32)], axis=1)
        rows_c = jnp.concatenate([m[l, bsz:bsz + 1, 3 * s:3 * s + 3], ln[None],
                                  jnp.zeros((1, 3, d), F32)], axis=1)
        return rows_l, rows_c

    xl, xc = x, ctx
    for l in range(depth):
        ctx_out = l < depth - 1
        i = l // 2
        m0l, m0c = mod_rows(l, 0)
        m1l, m1c = mod_rows(l, 1)
        m2l, m2c = mod_rows(l, 2)
        xl = _ffn(xl, m0l, wgu, wd, l, 0)
        xc = _ffn(xc, m0c, wgu, wd, l, 0)
        if l % 2 == 0:
            pc = _ab_prep(xc, m1c, ab_w_pad, ab_w_gate_t, conv_w, gate_p, gate_pt, i, None)
            pl_ = _ab_prep(xl, m1l, ab_w_pad, ab_w_gate_t, conv_w, gate_p, gate_pt, i, tab_a)
            aqc, akc, avc, bqc, bkc, bvc, zc, gcc, grc = pc
            aql, akl, avl, bql, bkl, bvl, zl, gcl, grl = pl_
            ol_a = _win_attn(aql, akc, avc, sink, i, akl, avl)
            s0 = jnp.zeros((bsz, N_CHAIN, B_DK, B_DV), F32)
            oc_f, oc_b, s_ctx = _delta(bqc, bkc, bvc, gcc, grc, s0)
            ol_f, ol_b, _ = _delta(bql, bkl, bvl, gcl, grl, s_ctx)
            xl = _ab_out(xl, m1l, ol_a, ol_f, ol_b, zl, gnorm, ab_wo, i)
            if ctx_out:
                oc_a = _win_attn(aqc, akc, avc, sink, i)
                xc = _ab_out(xc, m1c, oc_a, oc_f, oc_b, zc, gnorm, ab_wo, i)
        else:
            qc, kc, vc = _c_prep(xc, m1c, c_wi, i, qn[i], kn[i], None)
            ql, kl, vl = _c_prep(xl, m1l, c_wi, i, qn[i], kn[i], tab_c)
            ol = _flash(ql, kc, vc, kl, vl)
            xl = _c_out(xl, m1l, ol, c_wo, i)
            if ctx_out:
                oc = _flash(qc, kc, vc)
                xc = _c_out(xc, m1c, oc, c_wo, i)
        xl = _ffn(xl, m2l, wgu, wd, l, 2 - 1)
        if ctx_out:
            xc = _ffn(xc, m2c, wgu, wd, l, 1)
    return xl
```

```python
import functools

import jax
import jax.numpy as jnp
from jax import lax
from jax.experimental import pallas as pl
from jax.experimental.pallas import tpu as pltpu

F32 = jnp.float32
BF16 = jnp.bfloat16

D_MODEL = 1024
DEPTH = 4
GRID_W = 64
D_FF = 2816
N_SUB = 3
N_MOD = 3 * N_SUB
FFN_HALF = 0.5
NORM_EPS = 1e-6
ROPE_THETA = 10000.0
DEEP_ALPHA = (2 * DEPTH) ** 0.25

A_HEADS, A_KV, A_HD = 8, 2, 64
WINDOW = 128
BLOCK = 128
B_HEADS, B_DK, B_DV = 4, 128, 128
CONV_K = 5
CHUNK = 64
C_HEADS, C_KV, C_HD = 8, 2, 128

A_Q = A_HEADS * A_HD
A_KVW = A_KV * A_HD
B_QK = B_HEADS * B_DK
B_VW = B_HEADS * B_DV
B_QKV = 2 * B_QK + B_VW
AB_IN = A_Q + 2 * A_KVW + B_QKV + B_VW + 4 * B_HEADS
N_GATE = 4 * B_HEADS
LANE = 128
AB_IN_PAD = AB_IN - N_GATE + LANE
OFF_AK = A_Q
OFF_AV = A_Q + A_KVW
OFF_BQKV = A_Q + 2 * A_KVW
OFF_BZ = OFF_BQKV + B_QKV
OFF_GATE = OFF_BZ + B_VW
C_QW = C_HEADS * C_HD
C_KW = C_KV * C_HD
C_IN = C_QW + 2 * C_KW
HALO = 8
NEG = -1e30
MIB = 1024 * 1024


def _cparams(sem, vmem_mib):
    return pltpu.CompilerParams(dimension_semantics=sem, vmem_limit_bytes=vmem_mib * MIB)


def _dot(a, b):
    return jnp.dot(a, b, preferred_element_type=F32)


def _dot_nt(a, b):
    return lax.dot_general(a, b, (((1,), (1,)), ((), ())), preferred_element_type=F32)


def _silu(x):
    return x * jax.nn.sigmoid(x)


def _res_ln(x, y, gate, g, b):
    z = DEEP_ALPHA * x + gate * y
    mu = jnp.mean(z, axis=-1, keepdims=True)
    zc = z - mu
    var = jnp.mean(zc * zc, axis=-1, keepdims=True)
    return zc * lax.rsqrt(var + NORM_EPS) * g + b


def _token_tile(t):
    return 512 if t % 512 == 0 else 256


def _mod_body(c_ref, w_ref, b_ref, o_ref):
    c = c_ref[...]
    o_ref[...] = jnp.dot(_silu(c), w_ref[...], preferred_element_type=F32,
                         precision=lax.Precision.HIGHEST) + b_ref[...]


def _modulation(cvec, ada_w, ada_b):
    depth, d, n = ada_w.shape
    rows = cvec.shape[0]
    tn = 2304
    return pl.pallas_call(
        _mod_body,
        grid=(depth, n // tn),
        in_specs=[pl.BlockSpec((rows, d), lambda l, j: (0, 0)),
                  pl.BlockSpec((None, d, tn), lambda l, j: (l, 0, j)),
                  pl.BlockSpec((None, 1, tn), lambda l, j: (l, 0, j))],
        out_specs=pl.BlockSpec((None, rows, tn), lambda l, j: (l, 0, j)),
        out_shape=jax.ShapeDtypeStruct((depth, rows, n), F32),
        compiler_params=_cparams(("parallel", "parallel"), 40),
        name="mod",
    )(cvec, ada_w, ada_b.reshape(depth, 1, n))


FFN_CHUNKS = 2


def _ffn_body(x_ref, mod_ref, wgu_ref, wd_ref, o_ref):
    x = x_ref[...]
    mod = mod_ref[...]
    h = (x * (1.0 + mod[1:2]) + mod[0:1]).astype(BF16)
    fc = D_FF // FFN_CHUNKS
    y = None
    for j in range(FFN_CHUNKS):
        gt = _dot(h, wgu_ref[:, j * fc:(j + 1) * fc])
        up = _dot(h, wgu_ref[:, D_FF + j * fc:D_FF + (j + 1) * fc])
        act = (_silu(gt) * up).astype(BF16)
        part = _dot(act, wd_ref[j * fc:(j + 1) * fc, :])
        y = part if y is None else y + part
    o_ref[...] = _res_ln(x, y, FFN_HALF * mod[2:3], mod[3:4], mod[4:5])


def _ffn(x, mod, wgu, wd, l, s):
    bsz, t, d = x.shape
    tm = _token_tile(t)
    per_batch = mod.shape[0] > 1
    return pl.pallas_call(
        _ffn_body,
        grid=(bsz, t // tm),
        in_specs=[pl.BlockSpec((None, tm, d), lambda b, i: (b, i, 0)),
                  pl.BlockSpec((None, 8, d), lambda b, i: (b if per_batch else 0, 0, 0)),
                  pl.BlockSpec((None, None, d, 2 * D_FF), lambda b, i: (l, s, 0, 0),
                               pipeline_mode=pl.Buffered(1)),
                  pl.BlockSpec((None, None, D_FF, d), lambda b, i: (l, s, 0, 0),
                               pipeline_mode=pl.Buffered(1))],
        out_specs=pl.BlockSpec((None, tm, d), lambda b, i: (b, i, 0)),
        out_shape=jax.ShapeDtypeStruct(x.shape, F32),
        compiler_params=_cparams(("parallel", "parallel"), 52),
        name="ffn",
    )(x, mod, wgu, wd)


def _rope_full(x, cos2, sin2):
    return x * cos2 + pltpu.roll(x, C_HD // 2, 1) * sin2


def _c_prep_body(*refs, rope):
    if rope:
        x_ref, mod_ref, w_ref, qn_ref, kn_ref, cos_ref, sin_ref, q_ref, k_ref, v_ref = refs
        cos2, sin2 = cos_ref[...], sin_ref[...]
    else:
        x_ref, mod_ref, w_ref, qn_ref, kn_ref, q_ref, k_ref, v_ref = refs
    x = x_ref[...]
    mod = mod_ref[...]
    h = (x * (1.0 + mod[1:2]) + mod[0:1]).astype(BF16)
    qkv = _dot(h, w_ref[...])
    qn = qn_ref[...] * (C_HD ** -0.5 * LOG2E)
    kn = kn_ref[...]

    def head(col, gain):
        a = qkv[:, col:col + C_HD]
        a = a * lax.rsqrt(jnp.mean(a * a, axis=-1, keepdims=True) + NORM_EPS) * gain
        if rope:
            a = _rope_full(a, cos2, sin2)
        return a.astype(BF16)

    for j in range(C_HEADS):
        q_ref[:, j * C_HD:(j + 1) * C_HD] = head(j * C_HD, qn)
    for j in range(C_KV):
        k_ref[:, j * C_HD:(j + 1) * C_HD] = head(C_QW + j * C_HD, kn)
    v_ref[...] = qkv[:, C_QW + C_KW:].astype(BF16)


def _c_prep(x, mod, w_in, l, qn, kn, tables):
    bsz, t, d = x.shape
    tm = _token_tile(t)
    per_batch = mod.shape[0] > 1
    rope = tables is not None
    in_specs = [pl.BlockSpec((None, tm, d), lambda b, i: (b, i, 0)),
                pl.BlockSpec((None, 8, d), lambda b, i: (b if per_batch else 0, 0, 0)),
                pl.BlockSpec((None, d, C_IN), lambda b, i: (l, 0, 0), pipeline_mode=pl.Buffered(1)),
                pl.BlockSpec((1, C_HD), lambda b, i: (0, 0)),
                pl.BlockSpec((1, C_HD), lambda b, i: (0, 0))]
    args = [x, mod, w_in, qn, kn]
    if rope:
        in_specs += [pl.BlockSpec((tm, C_HD), lambda b, i: (i, 0))] * 2
        args += list(tables)
    return pl.pallas_call(
        functools.partial(_c_prep_body, rope=rope),
        grid=(bsz, t // tm),
        in_specs=in_specs,
        out_specs=[pl.BlockSpec((None, tm, C_QW), lambda b, i: (b, i, 0)),
                   pl.BlockSpec((None, tm, C_KW), lambda b, i: (b, i, 0)),
                   pl.BlockSpec((None, tm, C_KW), lambda b, i: (b, i, 0))],
        out_shape=[jax.ShapeDtypeStruct((bsz, t, C_QW), BF16),
                   jax.ShapeDtypeStruct((bsz, t, C_KW), BF16),
                   jax.ShapeDtypeStruct((bsz, t, C_KW), BF16)],
        compiler_params=_cparams(("parallel", "parallel"), 40),
        name="c_prep",
    )(*args)


C_GROUP = C_HEADS // C_KV
TQ_C = 128
KC_C = 256
RB_C = 64
FLASH_UNROLL = 4
LOG2E = 1.4426950408889634


def _flash_body(q_ref, k_ref, v_ref, o_ref, q_scr, s_scr, p0_scr, p1_scr, m_scr, mb_scr, l_scr, acc_scr, *, nch):
    i = pl.program_id(2)
    slot_a = i % 2
    slot_b = 1 - slot_a
    rows = q_scr.shape[0]
    tq = q_ref.shape[0]

    @pl.when(i == 0)
    def _():
        s_scr[1] = jnp.zeros(s_scr.shape[1:], F32)
        m_scr[1] = jnp.zeros(m_scr.shape[1:], F32)

    for g in range(C_GROUP):
        q_scr[g * tq:(g + 1) * tq, :] = q_ref[:, g * C_HD:(g + 1) * C_HD]
    m_cur = jnp.max(m_scr[slot_b], axis=-1, keepdims=True)
    mb_scr[...] = jnp.broadcast_to(m_cur, (rows, LANE))
    m_scr[slot_a] = jnp.full((rows, LANE), NEG, F32)
    l_scr[...] = jnp.zeros((rows, LANE), F32)
    acc_scr[...] = jnp.zeros((rows, C_HD), F32)

    def probs(j, p_scr):
        for r0 in range(0, rows, RB_C):
            mb = mb_scr[r0:r0 + RB_C, :]
            lt = l_scr[r0:r0 + RB_C, :]
            for c0 in range(0, KC_C, LANE):
                p = jnp.exp2(s_scr[slot_b, j, r0:r0 + RB_C, c0:c0 + LANE] - mb)
                lt = lt + p
                p_scr[r0:r0 + RB_C, c0:c0 + LANE] = p.astype(BF16)
            l_scr[r0:r0 + RB_C, :] = lt

    def pv(j, p_scr):
        k0 = pl.multiple_of(j * KC_C, KC_C)
        acc_scr[...] += _dot(p_scr[...], v_ref[pl.ds(k0, KC_C), :])

    def logits(j):
        k0 = pl.multiple_of(j * KC_C, KC_C)
        s = _dot_nt(q_scr[...], k_ref[pl.ds(k0, KC_C), :])
        s_scr[slot_a, j] = s
        mt = s[:, 0:LANE]
        for c0 in range(LANE, KC_C, LANE):
            mt = jnp.maximum(mt, s[:, c0:c0 + LANE])
        m_scr[slot_a] = jnp.maximum(m_scr[slot_a], mt)

    probs(0, p0_scr)

    def pair(t, carry):
        j = 2 * t
        logits(j)
        pv(j, p0_scr)
        probs(j + 1, p1_scr)
        logits(j + 1)
        pv(j + 1, p1_scr)
        probs(j + 2, p0_scr)
        return carry

    lax.fori_loop(0, (nch - 1) // 2, pair, 0, unroll=FLASH_UNROLL)
    pv(nch - 1, p0_scr)
    logits(nch - 1)

    o = acc_scr[...] / jnp.sum(l_scr[...], axis=-1, keepdims=True)
    for g in range(C_GROUP):
        o_ref[:, g * C_HD:(g + 1) * C_HD] = o[g * tq:(g + 1) * tq].astype(BF16)


def _flash(q, k, v):
    bsz, t, _ = q.shape
    n = k.shape[1]
    nch = n // KC_C
    assert n % KC_C == 0 and nch % 2 == 1
    tq = TQ_C
    nq = t // tq
    rows = C_GROUP * tq
    gw = C_GROUP * C_HD
    return pl.pallas_call(
        functools.partial(_flash_body, nch=nch),
        grid=(bsz, C_KV, nq + 1),
        in_specs=[pl.BlockSpec((None, tq, gw), lambda b, h, i: (b, jnp.minimum(i, nq - 1), h)),
                  pl.BlockSpec((None, n, C_HD), lambda b, h, i: (b, 0, h)),
                  pl.BlockSpec((None, n, C_HD), lambda b, h, i: (b, 0, h))],
        out_specs=pl.BlockSpec((None, tq, gw), lambda b, h, i: (b, jnp.maximum(i - 1, 0), h)),
        out_shape=jax.ShapeDtypeStruct((bsz, t, C_QW), BF16),
        scratch_shapes=[pltpu.VMEM((rows, C_HD), BF16),
                        pltpu.VMEM((2, nch, rows, KC_C), F32),
                        pltpu.VMEM((rows, KC_C), BF16),
                        pltpu.VMEM((rows, KC_C), BF16),
                        pltpu.VMEM((2, rows, LANE), F32),
                        pltpu.VMEM((rows, LANE), F32),
                        pltpu.VMEM((rows, LANE), F32),
                        pltpu.VMEM((rows, C_HD), F32)],
        compiler_params=_cparams(("parallel", "parallel", "arbitrary"), 48),
        name="flash",
    )(q, k, v)


def _c_out_body(x_ref, mod_ref, o_ref, w_ref, out_ref):
    mod = mod_ref[...]
    y = _dot(o_ref[...], w_ref[...])
    out_ref[...] = _res_ln(x_ref[...], y, mod[2:3], mod[3:4], mod[4:5])


def _c_out(x, mod, o, w_out, l):
    bsz, t, d = x.shape
    tm = _token_tile(t)
    per_batch = mod.shape[0] > 1
    return pl.pallas_call(
        _c_out_body,
        grid=(bsz, t // tm),
        in_specs=[pl.BlockSpec((None, tm, d), lambda b, i: (b, i, 0)),
                  pl.BlockSpec((None, 8, d), lambda b, i: (b if per_batch else 0, 0, 0)),
                  pl.BlockSpec((None, tm, C_QW), lambda b, i: (b, i, 0)),
                  pl.BlockSpec((None, C_QW, d), lambda b, i: (l, 0, 0), pipeline_mode=pl.Buffered(1))],
        out_specs=pl.BlockSpec((None, tm, d), lambda b, i: (b, i, 0)),
        out_shape=jax.ShapeDtypeStruct(x.shape, F32),
        compiler_params=_cparams(("parallel", "parallel"), 32),
        name="c_out",
    )(x, mod, o, w_out)


def _rope_half_tile(x, cos2, sin2, lane_lo):
    half = A_HD // 2
    partner = jnp.where(lane_lo, pltpu.roll(x, LANE - half, 1), pltpu.roll(x, half, 1))
    return x * cos2 + partner * sin2


def _gate_act(raw, is_beta, a_log, dt_bias):
    sp_in = raw + dt_bias
    softplus = jnp.maximum(sp_in, 0.0) + jnp.log(1.0 + jnp.exp(-jnp.abs(sp_in)))
    return jnp.where(is_beta, jax.nn.sigmoid(raw), -jnp.exp(a_log) * softplus)


def _ab_prep_body(*refs, rope, tm):
    if rope:
        (xp_ref, x_ref, xn_ref, mod_ref, w_ref, wgt_ref, cw_ref, gp_ref, gpt_ref, cos_ref, sin_ref,
         aq_ref, ak_ref, av_ref, bq_ref, bk_ref, bv_ref, z_ref, gc_ref, gr_ref, cbuf) = refs
    else:
        (xp_ref, x_ref, xn_ref, mod_ref, w_ref, wgt_ref, cw_ref, gp_ref, gpt_ref,
         aq_ref, ak_ref, av_ref, bq_ref, bk_ref, bv_ref, z_ref, gc_ref, gr_ref, cbuf) = refs
    i = pl.program_id(1)
    nt = pl.num_programs(1)
    mod = mod_ref[...]
    scale1, shift = 1.0 + mod[1:2], mod[0:1]
    h = (x_ref[...] * scale1 + shift).astype(BF16)
    proj = _dot(h, w_ref[...])

    if rope:
        cos2, sin2 = cos_ref[...], sin_ref[...]
        lane_lo = (lax.broadcasted_iota(jnp.int32, (tm, LANE), 1) % A_HD) < (A_HD // 2)
    for j in range(A_Q // LANE):
        a = proj[:, j * LANE:(j + 1) * LANE]
        if rope:
            a = _rope_half_tile(a, cos2, sin2, lane_lo)
        aq_ref[:, j * LANE:(j + 1) * LANE] = (a * (A_HD ** -0.5)).astype(BF16)
    a = proj[:, OFF_AK:OFF_AK + LANE]
    if rope:
        a = _rope_half_tile(a, cos2, sin2, lane_lo)
    ak_ref[...] = a.astype(BF16)
    av_ref[...] = proj[:, OFF_AV:OFF_AV + LANE].astype(BF16)
    z_ref[...] = proj[:, OFF_BZ:OFF_BZ + B_VW]

    gp = gp_ref[...]
    lane = lax.broadcasted_iota(jnp.int32, (1, LANE), 1)
    gc_ref[...] = _gate_act(proj[:, OFF_GATE:OFF_GATE + LANE], lane < 2 * B_HEADS, gp[0:1], gp[1:2])
    gpt = gpt_ref[...]
    row = lax.broadcasted_iota(jnp.int32, (N_GATE, 1), 0)
    gr_ref[...] = _gate_act(_dot_nt(wgt_ref[...], h), row < 2 * B_HEADS, gpt[:, 0:1], gpt[:, 1:2])

    w_b = w_ref[:, OFF_BQKV:OFF_BQKV + B_QKV]
    hp = (xp_ref[...] * scale1 + shift).astype(BF16)
    hn = (xn_ref[...] * scale1 + shift).astype(BF16)
    cbuf[0:HALO, :] = _dot(hp, w_b) * jnp.where(i > 0, 1.0, 0.0)
    cbuf[HALO:HALO + tm, :] = proj[:, OFF_BQKV:OFF_BQKV + B_QKV]
    cbuf[HALO + tm:, :] = _dot(hn, w_b) * jnp.where(i < nt - 1, 1.0, 0.0)
    cw = cw_ref[...]
    pad = CONV_K // 2
    for part, dst in enumerate((bq_ref, bk_ref, bv_ref)):
        for hd in range(B_HEADS):
            c0 = part * B_QK + hd * B_DK
            acc = None
            for j in range(CONV_K):
                term = cbuf[HALO - pad + j:HALO - pad + j + tm, c0:c0 + B_DK] * cw[j:j + 1, c0:c0 + B_DK]
                acc = term if acc is None else acc + term
            acc = _silu(acc)
            if part < 2:
                acc = acc * lax.rsqrt(jnp.sum(acc * acc, axis=-1, keepdims=True) + NORM_EPS)
            if part == 0:
                acc = acc * (B_DK ** -0.5)
            dst[:, hd * B_DK:(hd + 1) * B_DK] = acc


def _ab_prep(x, mod, w_pad, w_gate_t, conv_w, gate_p, gate_pt, l, tables):
    bsz, t, d = x.shape
    tm = _token_tile(t)
    nh = tm // HALO
    nblk = t // HALO
    per_batch = mod.shape[0] > 1
    rope = tables is not None
    const = dict(pipeline_mode=pl.Buffered(1))
    in_specs = [pl.BlockSpec((None, HALO, d), lambda b, i: (b, jnp.maximum(i * nh - 1, 0), 0)),
                pl.BlockSpec((None, tm, d), lambda b, i: (b, i, 0)),
                pl.BlockSpec((None, HALO, d), lambda b, i: (b, jnp.minimum((i + 1) * nh, nblk - 1), 0)),
                pl.BlockSpec((None, 8, d), lambda b, i: (b if per_batch else 0, 0, 0)),
                pl.BlockSpec((None, d, AB_IN_PAD), lambda b, i: (l, 0, 0), **const),
                pl.BlockSpec((None, N_GATE, d), lambda b, i: (l, 0, 0), **const),
                pl.BlockSpec((None, 8, B_QKV), lambda b, i: (l, 0, 0)),
                pl.BlockSpec((None, 8, LANE), lambda b, i: (l, 0, 0)),
                pl.BlockSpec((None, N_GATE, LANE), lambda b, i: (l, 0, 0))]
    args = [x, x, x, mod, w_pad, w_gate_t, conv_w, gate_p, gate_pt]
    if rope:
        in_specs += [pl.BlockSpec((tm, LANE), lambda b, i: (i, 0))] * 2
        args += list(tables)
    tok = lambda w: pl.BlockSpec((None, tm, w), lambda b, i: (b, i, 0))
    shp = lambda w, dt: jax.ShapeDtypeStruct((bsz, t, w), dt)
    return pl.pallas_call(
        functools.partial(_ab_prep_body, rope=rope, tm=tm),
        grid=(bsz, t // tm),
        in_specs=in_specs,
        out_specs=[tok(A_Q), tok(A_KVW), tok(A_KVW), tok(B_QK), tok(B_QK), tok(B_VW), tok(B_VW), tok(LANE),
                   pl.BlockSpec((None, N_GATE, tm), lambda b, i: (b, 0, i))],
        out_shape=[shp(A_Q, BF16), shp(A_KVW, BF16), shp(A_KVW, BF16), shp(B_QK, F32), shp(B_QK, F32),
                   shp(B_VW, F32), shp(B_VW, F32), shp(LANE, F32),
                   jax.ShapeDtypeStruct((bsz, N_GATE, t), F32)],
        scratch_shapes=[pltpu.VMEM((tm + 2 * HALO, B_QKV), F32)],
        compiler_params=_cparams(("parallel", "parallel"), 52),
        name="ab_prep",
    )(*args)


A_GROUP = A_HEADS // A_KV


def _win_body(*refs, local):
    if local:
        q_ref, kp_ref, kc_ref, kn_ref, vp_ref, vc_ref, vn_ref, kx_ref, vx_ref, sink_ref, o_ref = refs
    else:
        q_ref, kx_ref, vx_ref, sink_ref, o_ref = refs
    tq = q_ref.shape[0]
    rows = A_GROUP * tq
    if local:
        i = pl.program_id(1)
        nb = pl.num_programs(1)
        r = lax.broadcasted_iota(jnp.int32, (rows, 3 * BLOCK), 0) % tq
        c = lax.broadcasted_iota(jnp.int32, (rows, 3 * BLOCK), 1)
        valid = jnp.abs(c - r - BLOCK) <= WINDOW
        valid = valid & ((c >= BLOCK) | (i > 0)) & ((c < 2 * BLOCK) | (i < nb - 1))
        k3 = jnp.concatenate([kp_ref[...], kc_ref[...], kn_ref[...]], axis=0)
        v3 = jnp.concatenate([vp_ref[...], vc_ref[...], vn_ref[...]], axis=0)
    kx = kx_ref[...]
    vx = vx_ref[...]
    sink = sink_ref[...]
    for kv in range(A_KV):
        qs = jnp.concatenate(
            [q_ref[:, (kv * A_GROUP + g) * A_HD:(kv * A_GROUP + g + 1) * A_HD] for g in range(A_GROUP)], axis=0)
        sk = jnp.concatenate(
            [jnp.broadcast_to(sink[kv * A_GROUP + g:kv * A_GROUP + g + 1, 0:1], (tq, 1)) for g in range(A_GROUP)],
            axis=0)
        lo, hi = kv * A_HD, (kv + 1) * A_HD
        s_x = _dot_nt(qs, kx[:, lo:hi])
        m = jnp.maximum(jnp.max(s_x, axis=-1, keepdims=True), sk)
        if local:
            s_l = jnp.where(valid, _dot_nt(qs, k3[:, lo:hi]), NEG)
            m = jnp.maximum(m, jnp.max(s_l, axis=-1, keepdims=True))
        p_x = jnp.exp(s_x - m)
        den = jnp.sum(p_x, axis=-1, keepdims=True) + jnp.exp(sk - m)
        o = _dot(p_x.astype(BF16), vx[:, lo:hi])
        if local:
            p_l = jnp.exp(s_l - m)
            den = den + jnp.sum(p_l, axis=-1, keepdims=True)
            o = o + _dot(p_l.astype(BF16), v3[:, lo:hi])
        o = o / den
        for g in range(A_GROUP):
            hq = kv * A_GROUP + g
            o_ref[:, hq * A_HD:(hq + 1) * A_HD] = o[g * tq:(g + 1) * tq].astype(BF16)


def _win_attn(q, kx, vx, sink, l, k=None, v=None):
    bsz, t, _ = q.shape
    lc = kx.shape[1]
    local = k is not None
    nb = t // BLOCK
    blk = lambda f: pl.BlockSpec((None, BLOCK, A_KVW), f)
    in_specs = [pl.BlockSpec((None, BLOCK, A_Q), lambda b, i: (b, i, 0))]
    args = [q]
    if local:
        prv = lambda b, i: (b, jnp.maximum(i - 1, 0), 0)
        cur = lambda b, i: (b, i, 0)
        nxt = lambda b, i: (b, jnp.minimum(i + 1, nb - 1), 0)
        in_specs += [blk(prv), blk(cur), blk(nxt), blk(prv), blk(cur), blk(nxt)]
        args += [k, k, k, v, v, v]
    in_specs += [pl.BlockSpec((None, lc, A_KVW), lambda b, i: (b, 0, 0)),
                 pl.BlockSpec((None, lc, A_KVW), lambda b, i: (b, 0, 0)),
                 pl.BlockSpec((None, A_HEADS, LANE), lambda b, i: (l, 0, 0))]
    args += [kx, vx, sink]
    return pl.pallas_call(
        functools.partial(_win_body, local=local),
        grid=(bsz, nb),
        in_specs=in_specs,
        out_specs=pl.BlockSpec((None, BLOCK, A_Q), lambda b, i: (b, i, 0)),
        out_shape=jax.ShapeDtypeStruct((bsz, t, A_Q), BF16),
        compiler_params=_cparams(("parallel", "parallel"), 32),
        name="win_attn",
    )(*args)


DELTA_BLOCK = 2 * CHUNK
N_CHAIN = 2 * B_HEADS


def _split_bf16(a):
    hi = a.astype(BF16)
    return hi, (a - hi.astype(F32)).astype(BF16)


def _delta_intra(units, eye):
    for un in units:
        decay = jnp.exp(jnp.where(un["incl"], un["gcc"] - un["gcr"], NEG))
        kb = un["k"] * un["beta"]
        a = _dot_nt(jnp.concatenate([kb, un["q"]], axis=0).astype(BF16), un["k"].astype(BF16))
        un["p"] = jnp.where(un["strict"], a[:CHUNK] * decay, 0.0) * -1.0
        un["attn"] = (a[CHUNK:] * decay).astype(BF16)
        un["t"] = eye + un["p"]
        eg = jnp.exp(un["gcc"])
        un["rhs"] = jnp.concatenate([un["v"] * un["beta"], kb * eg], axis=1).astype(BF16)
        un["qd"] = un["q"] * eg
        un["kt_t"] = (un["k"] * jnp.exp(un["gtot"] - un["gcc"])).T.astype(BF16)
        un["dec"] = jnp.exp(un["gtot"])
    for _ in range(CHUNK.bit_length() - 2):
        for un in units:
            pb = un["p"].astype(BF16)
            un["p"] = _dot(pb, pb)
        for un in units:
            un["t"] = un["t"] + _dot(un["t"].astype(BF16), un["p"].astype(BF16))
    for un in units:
        uw = _dot(un["t"].astype(BF16), un["rhs"])
        un["u"] = uw[:, :B_DV]
        un["wq"] = jnp.concatenate([uw[:, B_DV:], un["qd"]], axis=0).astype(BF16)


def _delta_body(qf_ref, kf_ref, vf_ref, gcf_ref, grf_ref, qb_ref, kb_ref, vb_ref, gcb_ref, grb_ref, s0_ref,
                of_ref, ob_ref, s_ref):
    @pl.when(pl.program_id(1) == 0)
    def _():
        s_ref[...] = s0_ref[...]

    li = lax.broadcasted_iota(jnp.int32, (CHUNK, CHUNK), 0)
    lj = lax.broadcasted_iota(jnp.int32, (CHUNK, CHUNK), 1)
    eye = jnp.where(li == lj, 1.0, 0.0)
    tri_l = jnp.where(li >= lj, 1.0, 0.0).astype(BF16)
    tri_u = jnp.where(li <= lj, 1.0, 0.0).astype(BF16)

    dirs = ((qf_ref, kf_ref, vf_ref, gcf_ref, grf_ref, of_ref, (0, 1), tri_l, tri_u, li >= lj, li > lj, CHUNK - 1),
            (qb_ref, kb_ref, vb_ref, gcb_ref, grb_ref, ob_ref, (1, 0), tri_u, tri_l, li <= lj, li < lj, 0))
    units = {}
    for d, (q_ref, k_ref, v_ref, gc_ref, gr_ref, o_ref, order, tri_c, tri_r, incl, strict, last) in enumerate(dirs):
        for c in order:
            r0 = c * CHUNK
            gcol = gc_ref[r0:r0 + CHUNK, :]
            ghi, glo = _split_bf16(gcol)
            cum_c = _dot(tri_c, ghi) + _dot(tri_c, glo)
            grow = gr_ref[:, r0:r0 + CHUNK]
            rhi, rlo = _split_bf16(grow)
            cum_r = _dot(rhi, tri_r) + _dot(rlo, tri_r)
            for hd in range(B_HEADS):
                col = d * B_HEADS + hd
                gl = 2 * B_HEADS + col
                lo, hi = hd * B_DK, (hd + 1) * B_DK
                units[(d, c, hd)] = dict(
                    q=q_ref[r0:r0 + CHUNK, lo:hi], k=k_ref[r0:r0 + CHUNK, lo:hi], v=v_ref[r0:r0 + CHUNK, lo:hi],
                    beta=gcol[:, col:col + 1], gcc=cum_c[:, gl:gl + 1], gcr=cum_r[gl:gl + 1, :],
                    gtot=cum_c[last:last + 1, gl:gl + 1], incl=incl, strict=strict)
    _delta_intra(list(units.values()), eye)

    chains = [(d, hd) for d in range(2) for hd in range(B_HEADS)]
    state = [s_ref[d * B_HEADS + hd] for d, hd in chains]
    for step in range(2):
        cur = [units[(d, dirs[d][6][step], hd)] for d, hd in chains]
        ws = [_dot(un["wq"], s.astype(BF16)) for un, s in zip(cur, state)]
        v_new = [(un["u"] - w[:CHUNK]).astype(BF16) for un, w in zip(cur, ws)]
        outs = [w[CHUNK:] + _dot(un["attn"], vn) for un, w, vn in zip(cur, ws, v_new)]
        state = [s * un["dec"] + _dot(un["kt_t"], vn) for un, s, vn in zip(cur, state, v_new)]
        for (d, hd), o in zip(chains, outs):
            r0 = dirs[d][6][step] * CHUNK
            dirs[d][5][r0:r0 + CHUNK, hd * B_DV:(hd + 1) * B_DV] = o
    for (d, hd), s in zip(chains, state):
        s_ref[d * B_HEADS + hd] = s


def _delta(bq, bk, bv, gcol, grow, s0):
    bsz, t, _ = bq.shape
    ns = t // DELTA_BLOCK
    fwd = lambda w: pl.BlockSpec((None, DELTA_BLOCK, w), lambda b, s: (b, s, 0))
    bwd = lambda w: pl.BlockSpec((None, DELTA_BLOCK, w), lambda b, s: (b, ns - 1 - s, 0))
    st = pl.BlockSpec((None, N_CHAIN, B_DK, B_DV), lambda b, s: (b, 0, 0, 0))
    return pl.pallas_call(
        _delta_body,
        grid=(bsz, ns),
        in_specs=[fwd(B_QK), fwd(B_QK), fwd(B_VW), fwd(LANE),
                  pl.BlockSpec((None, N_GATE, DELTA_BLOCK), lambda b, s: (b, 0, s)),
                  bwd(B_QK), bwd(B_QK), bwd(B_VW), bwd(LANE),
                  pl.BlockSpec((None, N_GATE, DELTA_BLOCK), lambda b, s: (b, 0, ns - 1 - s)),
                  st],
        out_specs=[fwd(B_VW), bwd(B_VW), st],
        out_shape=[jax.ShapeDtypeStruct((bsz, t, B_VW), F32), jax.ShapeDtypeStruct((bsz, t, B_VW), F32),
                   jax.ShapeDtypeStruct((bsz, N_CHAIN, B_DK, B_DV), F32)],
        compiler_params=_cparams(("parallel", "arbitrary"), 32),
        name="delta",
    )(bq, bk, bv, gcol, grow, bq, bk, bv, gcol, grow, s0)


def _ab_out_body(x_ref, mod_ref, oa_ref, of_ref, ob_ref, z_ref, gn_ref, w_ref, out_ref):
    mod = mod_ref[...]
    gn = gn_ref[...]
    y = _dot(oa_ref[...], w_ref[0:A_Q, :])
    for hd in range(B_HEADS):
        lo, hi = hd * B_DV, (hd + 1) * B_DV
        o = of_ref[:, lo:hi] + ob_ref[:, lo:hi]
        o = o * lax.rsqrt(jnp.mean(o * o, axis=-1, keepdims=True) + NORM_EPS) * gn
        o = (o * _silu(z_ref[:, lo:hi])).astype(BF16)
        y = y + _dot(o, w_ref[A_Q + lo:A_Q + hi, :])
    out_ref[...] = _res_ln(x_ref[...], y, mod[2:3], mod[3:4], mod[4:5])


def _ab_out(x, mod, oa, of, ob, z, gnorm, w_out, l):
    bsz, t, d = x.shape
    tm = _token_tile(t)
    per_batch = mod.shape[0] > 1
    tok = lambda w: pl.BlockSpec((None, tm, w), lambda b, i: (b, i, 0))
    return pl.pallas_call(
        _ab_out_body,
        grid=(bsz, t // tm),
        in_specs=[tok(d),
                  pl.BlockSpec((None, 8, d), lambda b, i: (b if per_batch else 0, 0, 0)),
                  tok(A_Q), tok(B_VW), tok(B_VW), tok(B_VW),
                  pl.BlockSpec((None, 1, B_DV), lambda b, i: (l, 0, 0)),
                  pl.BlockSpec((None, A_Q + B_VW, d), lambda b, i: (l, 0, 0), pipeline_mode=pl.Buffered(1))],
        out_specs=tok(d),
        out_shape=jax.ShapeDtypeStruct(x.shape, F32),
        compiler_params=_cparams(("parallel", "parallel"), 40),
        name="ab_out",
    )(x, mod, oa, of, ob, z, gnorm, w_out)


def _rope_tables(rows, head_dim):
    n_freq = head_dim // 4
    inv = ROPE_THETA ** (-jnp.arange(n_freq, dtype=F32) / n_freq)
    r, col = jnp.meshgrid(jnp.arange(rows, dtype=F32), jnp.arange(GRID_W, dtype=F32), indexing='ij')
    r, col = r.reshape(-1), col.reshape(-1)
    ang = jnp.concatenate([r[:, None] * inv, col[:, None] * inv], axis=-1)
    cos, sin = jnp.cos(ang), jnp.sin(ang)
    cos2 = jnp.concatenate([cos, cos], axis=-1)
    sin2 = jnp.concatenate([-sin, sin], axis=-1)
    rep = LANE // head_dim
    return jnp.tile(cos2, (1, rep)), jnp.tile(sin2, (1, rep))


def _pad_rows(a, rows):
    return jnp.pad(a, ((0, 0), (0, rows - a.shape[1]), (0, 0)))


def kernel(x, c, ctx, c_ctx, ada_w, ada_b, ln_g, ln_b, ffn_w_gu, ffn_w_down, ab_w_in, ab_conv_w, ab_a_log,
           ab_dt_bias, ab_gnorm, ab_sink, ab_w_out, c_w_in, c_q_norm, c_k_norm, c_w_out):
    bsz, t, d = x.shape
    depth = ada_w.shape[0]
    n_even = ab_w_in.shape[0]
    rows = t // GRID_W
    tab_a = _rope_tables(rows, A_HD)
    tab_c = _rope_tables(rows, C_HD)

    wgu = ffn_w_gu.astype(BF16)
    wd = ffn_w_down.astype(BF16)
    ab_w_pad = jnp.pad(ab_w_in, ((0, 0), (0, 0), (0, AB_IN_PAD - AB_IN))).astype(BF16)
    ab_w_gate_t = jnp.swapaxes(ab_w_in[:, :, OFF_GATE:OFF_GATE + N_GATE], 1, 2).astype(BF16)
    ab_wo = ab_w_out.astype(BF16)
    c_wi = c_w_in.astype(BF16)
    c_wo = c_w_out.astype(BF16)
    conv_w = _pad_rows(ab_conv_w, 8)
    zeros8 = jnp.zeros((n_even, 2 * B_HEADS), F32)
    a_log16 = jnp.concatenate([zeros8, ab_a_log.reshape(n_even, 2 * B_HEADS)], axis=1)
    dtb16 = jnp.concatenate([zeros8, ab_dt_bias.reshape(n_even, 2 * B_HEADS)], axis=1)
    gate_p = _pad_rows(jnp.pad(jnp.stack([a_log16, dtb16], axis=1), ((0, 0), (0, 0), (0, LANE - N_GATE))), 8)
    gate_pt = jnp.pad(jnp.stack([a_log16, dtb16], axis=2), ((0, 0), (0, 0), (0, LANE - 2)))
    sink = jnp.broadcast_to(ab_sink[:, :, None], (n_even, A_HEADS, LANE))
    gnorm = ab_gnorm.reshape(n_even, 1, B_DV)
    qn = c_q_norm.reshape(-1, 1, C_HD)
    kn = c_k_norm.reshape(-1, 1, C_HD)

    nrow = ((bsz + 1 + 7) // 8) * 8
    cvec = jnp.pad(jnp.concatenate([c, c_ctx[None]], axis=0), ((0, nrow - bsz - 1), (0, 0)))
    m = _modulation(cvec, ada_w, ada_b).reshape(depth, nrow, N_MOD, d)

    def mod_rows(l, s):
        ln = jnp.stack([ln_g[l, s], ln_b[l, s]], axis=0)
        rows_l = jnp.concatenate([m[l, :bsz, 3 * s:3 * s + 3], jnp.broadcast_to(ln, (bsz, 2, d)),
                                  jnp.zeros((bsz, 3, d), F32)], axis=1)
        rows_c = jnp.concatenate([m[l, bsz:bsz + 1, 3 * s:3 * s + 3], ln[None],
                                  jnp.zeros((1, 3, d), F32)], axis=1)
        return rows_l, rows_c

    xl, xc = x, ctx
    for l in range(depth):
        ctx_out = l < depth - 1
        i = l // 2
        m0l, m0c = mod_rows(l, 0)
        m1l, m1c = mod_rows(l, 1)
        m2l, m2c = mod_rows(l, 2)
        xl = _ffn(xl, m0l, wgu, wd, l, 0)
        xc = _ffn(xc, m0c, wgu, wd, l, 0)
        if l % 2 == 0:
            pc = _ab_prep(xc, m1c, ab_w_pad, ab_w_gate_t, conv_w, gate_p, gate_pt, i, None)
            pl_ = _ab_prep(xl, m1l, ab_w_pad, ab_w_gate_t, conv_w, gate_p, gate_pt, i, tab_a)
            aqc, akc, avc, bqc, bkc, bvc, zc, gcc, grc = pc
            aql, akl, avl, bql, bkl, bvl, zl, gcl, grl = pl_
            ol_a = _win_attn(aql, akc, avc, sink, i, akl, avl)
            s0 = jnp.zeros((bsz, N_CHAIN, B_DK, B_DV), F32)
            oc_f, oc_b, s_ctx = _delta(bqc, bkc, bvc, gcc, grc, s0)
            ol_f, ol_b, _ = _delta(bql, bkl, bvl, gcl, grl, s_ctx)
            xl = _ab_out(xl, m1l, ol_a, ol_f, ol_b, zl, gnorm, ab_wo, i)
            if ctx_out:
                oc_a = _win_attn(aqc, akc, avc, sink, i)
                xc = _ab_out(xc, m1c, oc_a, oc_f, oc_b, zc, gnorm, ab_wo, i)
        else:
            qc, kc, vc = _c_prep(xc, m1c, c_wi, i, qn[i], kn[i], None)
            ql, kl, vl = _c_prep(xl, m1l, c_wi, i, qn[i], kn[i], tab_c)
            ol = _flash(ql, jnp.concatenate([kc, kl], axis=1), jnp.concatenate([vc, vl], axis=1))
            xl = _c_out(xl, m1l, ol, c_wo, i)
            if ctx_out:
                oc = _flash(qc, kc, vc)
                xc = _c_out(xc, m1c, oc, c_wo, i)
        xl = _ffn(xl, m2l, wgu, wd, l, 2 - 1)
        if ctx_out:
            xc = _ffn(xc, m2c, wgu, wd, l, 1)
    return xl
```

```python
import functools

import jax
import jax.numpy as jnp
from jax import lax
from jax.experimental import pallas as pl
from jax.experimental.pallas import tpu as pltpu

F32 = jnp.float32
BF16 = jnp.bfloat16

D_MODEL = 1024
DEPTH = 4
GRID_W = 64
D_FF = 2816
N_SUB = 3
N_MOD = 3 * N_SUB
FFN_HALF = 0.5
NORM_EPS = 1e-6
ROPE_THETA = 10000.0
DEEP_ALPHA = (2 * DEPTH) ** 0.25

A_HEADS, A_KV, A_HD = 8, 2, 64
WINDOW = 128
BLOCK = 128
B_HEADS, B_DK, B_DV = 4, 128, 128
CONV_K = 5
CHUNK = 64
C_HEADS, C_KV, C_HD = 8, 2, 128

A_Q = A_HEADS * A_HD
A_KVW = A_KV * A_HD
B_QK = B_HEADS * B_DK
B_VW = B_HEADS * B_DV
B_QKV = 2 * B_QK + B_VW
AB_IN = A_Q + 2 * A_KVW + B_QKV + B_VW + 4 * B_HEADS
N_GATE = 4 * B_HEADS
LANE = 128
AB_IN_PAD = AB_IN - N_GATE + LANE
OFF_AK = A_Q
OFF_AV = A_Q + A_KVW
OFF_BQKV = A_Q + 2 * A_KVW
OFF_BZ = OFF_BQKV + B_QKV
OFF_GATE = OFF_BZ + B_VW
C_QW = C_HEADS * C_HD
C_KW = C_KV * C_HD
C_IN = C_QW + 2 * C_KW
HALO = 8
NEG = -1e30
MIB = 1024 * 1024


def _cparams(sem, vmem_mib):
    return pltpu.CompilerParams(dimension_semantics=sem, vmem_limit_bytes=vmem_mib * MIB)


def _dot(a, b):
    return jnp.dot(a, b, preferred_element_type=F32)


def _dot_nt(a, b):
    return lax.dot_general(a, b, (((1,), (1,)), ((), ())), preferred_element_type=F32)


def _silu(x):
    return x * jax.nn.sigmoid(x)


def _res_ln(x, y, gate, g, b):
    z = DEEP_ALPHA * x + gate * y
    mu = jnp.mean(z, axis=-1, keepdims=True)
    zc = z - mu
    var = jnp.mean(zc * zc, axis=-1, keepdims=True)
    return zc * lax.rsqrt(var + NORM_EPS) * g + b


def _token_tile(t):
    return 512 if t % 512 == 0 else 256


def _mod_body(c_ref, w_ref, b_ref, o_ref):
    c = c_ref[...]
    o_ref[...] = jnp.dot(_silu(c), w_ref[...], preferred_element_type=F32,
                         precision=lax.Precision.HIGHEST) + b_ref[...]


def _modulation(cvec, ada_w, ada_b):
    depth, d, n = ada_w.shape
    rows = cvec.shape[0]
    tn = 2304
    return pl.pallas_call(
        _mod_body,
        grid=(depth, n // tn),
        in_specs=[pl.BlockSpec((rows, d), lambda l, j: (0, 0)),
                  pl.BlockSpec((None, d, tn), lambda l, j: (l, 0, j)),
                  pl.BlockSpec((None, 1, tn), lambda l, j: (l, 0, j))],
        out_specs=pl.BlockSpec((None, rows, tn), lambda l, j: (l, 0, j)),
        out_shape=jax.ShapeDtypeStruct((depth, rows, n), F32),
        compiler_params=_cparams(("parallel", "parallel"), 40),
        name="mod",
    )(cvec, ada_w, ada_b.reshape(depth, 1, n))


FFN_CHUNKS = 2


def _ffn_body(x_ref, mod_ref, wgu_ref, wd_ref, o_ref):
    x = x_ref[...]
    mod = mod_ref[...]
    h = (x * (1.0 + mod[1:2]) + mod[0:1]).astype(BF16)
    fc = D_FF // FFN_CHUNKS
    y = None
    for j in range(FFN_CHUNKS):
        gt = _dot(h, wgu_ref[:, j * fc:(j + 1) * fc])
        up = _dot(h, wgu_ref[:, D_FF + j * fc:D_FF + (j + 1) * fc])
        act = (_silu(gt) * up).astype(BF16)
        part = _dot(act, wd_ref[j * fc:(j + 1) * fc, :])
        y = part if y is None else y + part
    o_ref[...] = _res_ln(x, y, FFN_HALF * mod[2:3], mod[3:4], mod[4:5])


def _ffn(x, mod, wgu, wd, l, s):
    bsz, t, d = x.shape
    tm = _token_tile(t)
    per_batch = mod.shape[0] > 1
    return pl.pallas_call(
        _ffn_body,
        grid=(bsz, t // tm),
        in_specs=[pl.BlockSpec((None, tm, d), lambda b, i: (b, i, 0)),
                  pl.BlockSpec((None, 8, d), lambda b, i: (b if per_batch else 0, 0, 0)),
                  pl.BlockSpec((None, None, d, 2 * D_FF), lambda b, i: (l, s, 0, 0),
                               pipeline_mode=pl.Buffered(1)),
                  pl.BlockSpec((None, None, D_FF, d), lambda b, i: (l, s, 0, 0),
                               pipeline_mode=pl.Buffered(1))],
        out_specs=pl.BlockSpec((None, tm, d), lambda b, i: (b, i, 0)),
        out_shape=jax.ShapeDtypeStruct(x.shape, F32),
        compiler_params=_cparams(("parallel", "parallel"), 52),
        name="ffn",
    )(x, mod, wgu, wd)


def _rope_full(x, cos2, sin2):
    return x * cos2 + pltpu.roll(x, C_HD // 2, 1) * sin2


def _c_prep_body(*refs, rope):
    if rope:
        x_ref, mod_ref, w_ref, qn_ref, kn_ref, cos_ref, sin_ref, q_ref, k_ref, v_ref = refs
        cos2, sin2 = cos_ref[...], sin_ref[...]
    else:
        x_ref, mod_ref, w_ref, qn_ref, kn_ref, q_ref, k_ref, v_ref = refs
    x = x_ref[...]
    mod = mod_ref[...]
    h = (x * (1.0 + mod[1:2]) + mod[0:1]).astype(BF16)
    qkv = _dot(h, w_ref[...])
    qn = qn_ref[...] * (C_HD ** -0.5 * LOG2E)
    kn = kn_ref[...]

    def head(col, gain):
        a = qkv[:, col:col + C_HD]
        a = a * lax.rsqrt(jnp.mean(a * a, axis=-1, keepdims=True) + NORM_EPS) * gain
        if rope:
            a = _rope_full(a, cos2, sin2)
        return a.astype(BF16)

    for j in range(C_HEADS):
        q_ref[:, j * C_HD:(j + 1) * C_HD] = head(j * C_HD, qn)
    for j in range(C_KV):
        k_ref[:, j * C_HD:(j + 1) * C_HD] = head(C_QW + j * C_HD, kn)
    v_ref[...] = qkv[:, C_QW + C_KW:].astype(BF16)


def _c_prep(x, mod, w_in, l, qn, kn, tables):
    bsz, t, d = x.shape
    tm = _token_tile(t)
    per_batch = mod.shape[0] > 1
    rope = tables is not None
    in_specs = [pl.BlockSpec((None, tm, d), lambda b, i: (b, i, 0)),
                pl.BlockSpec((None, 8, d), lambda b, i: (b if per_batch else 0, 0, 0)),
                pl.BlockSpec((None, d, C_IN), lambda b, i: (l, 0, 0), pipeline_mode=pl.Buffered(1)),
                pl.BlockSpec((1, C_HD), lambda b, i: (0, 0)),
                pl.BlockSpec((1, C_HD), lambda b, i: (0, 0))]
    args = [x, mod, w_in, qn, kn]
    if rope:
        in_specs += [pl.BlockSpec((tm, C_HD), lambda b, i: (i, 0))] * 2
        args += list(tables)
    return pl.pallas_call(
        functools.partial(_c_prep_body, rope=rope),
        grid=(bsz, t // tm),
        in_specs=in_specs,
        out_specs=[pl.BlockSpec((None, tm, C_QW), lambda b, i: (b, i, 0)),
                   pl.BlockSpec((None, tm, C_KW), lambda b, i: (b, i, 0)),
                   pl.BlockSpec((None, tm, C_KW), lambda b, i: (b, i, 0))],
        out_shape=[jax.ShapeDtypeStruct((bsz, t, C_QW), BF16),
                   jax.ShapeDtypeStruct((bsz, t, C_KW), BF16),
                   jax.ShapeDtypeStruct((bsz, t, C_KW), BF16)],
        compiler_params=_cparams(("parallel", "parallel"), 40),
        name="c_prep",
    )(*args)


C_GROUP = C_HEADS // C_KV
TQ_C = 128
KC_C = 256
RB_C = 64
FLASH_UNROLL = 4
LOG2E = 1.4426950408889634


def _flash_body(q_ref, k_ref, v_ref, o_ref, q_scr, s_scr, p0_scr, p1_scr, m_scr, mb_scr, l_scr, acc_scr, *,
                nch, tail_keys):
    i = pl.program_id(2)
    slot_a = i % 2
    slot_b = 1 - slot_a
    rows = q_scr.shape[0]
    tq = q_ref.shape[0]

    @pl.when(i == 0)
    def _():
        s_scr[1] = jnp.zeros(s_scr.shape[1:], F32)
        m_scr[1] = jnp.zeros(m_scr.shape[1:], F32)

    for g in range(C_GROUP):
        q_scr[g * tq:(g + 1) * tq, :] = q_ref[:, g * C_HD:(g + 1) * C_HD]
    m_cur = jnp.max(m_scr[slot_b], axis=-1, keepdims=True)
    mb_scr[...] = jnp.broadcast_to(m_cur, (rows, LANE))
    m_scr[slot_a] = jnp.full((rows, LANE), NEG, F32)
    l_scr[...] = jnp.zeros((rows, LANE), F32)
    acc_scr[...] = jnp.zeros((rows, C_HD), F32)

    def probs(j, p_scr):
        for r0 in range(0, rows, RB_C):
            mb = mb_scr[r0:r0 + RB_C, :]
            lt = l_scr[r0:r0 + RB_C, :]
            for c0 in range(0, KC_C, LANE):
                p = jnp.exp2(s_scr[slot_b, j, r0:r0 + RB_C, c0:c0 + LANE] - mb)
                lt = lt + p
                p_scr[r0:r0 + RB_C, c0:c0 + LANE] = p.astype(BF16)
            l_scr[r0:r0 + RB_C, :] = lt

    def pv(j, p_scr):
        k0 = pl.multiple_of(j * KC_C, KC_C)
        acc_scr[...] += _dot(p_scr[...], v_ref[pl.ds(k0, KC_C), :])

    def logits(j, n_keys=KC_C):
        k0 = pl.multiple_of(j * KC_C, KC_C)
        s = _dot_nt(q_scr[...], k_ref[pl.ds(k0, KC_C), :])
        if n_keys < KC_C:
            s = jnp.where(lax.broadcasted_iota(jnp.int32, (1, KC_C), 1) < n_keys, s, NEG)
        s_scr[slot_a, j] = s
        mt = s[:, 0:LANE]
        for c0 in range(LANE, KC_C, LANE):
            mt = jnp.maximum(mt, s[:, c0:c0 + LANE])
        m_scr[slot_a] = jnp.maximum(m_scr[slot_a], mt)

    probs(0, p0_scr)

    def pair(t, carry):
        j = 2 * t
        logits(j)
        pv(j, p0_scr)
        probs(j + 1, p1_scr)
        logits(j + 1)
        pv(j + 1, p1_scr)
        probs(j + 2, p0_scr)
        return carry

    lax.fori_loop(0, (nch - 1) // 2, pair, 0, unroll=FLASH_UNROLL)
    pv(nch - 1, p0_scr)
    logits(nch - 1, tail_keys)

    o = acc_scr[...] / jnp.sum(l_scr[...], axis=-1, keepdims=True)
    for g in range(C_GROUP):
        o_ref[:, g * C_HD:(g + 1) * C_HD] = o[g * tq:(g + 1) * tq].astype(BF16)


def _flash(q, ks, vs):
    bsz, t, _ = q.shape
    n_keys = sum(a.shape[1] for a in ks)
    nch = -(-n_keys // KC_C)
    nch += 1 - nch % 2
    n = nch * KC_C
    tail_keys = n_keys - (nch - 1) * KC_C
    assert 0 < tail_keys <= KC_C
    pad = [jnp.zeros((bsz, n - n_keys, C_KW), BF16)] if n > n_keys else []
    k = jnp.concatenate(list(ks) + pad, axis=1)
    v = jnp.concatenate(list(vs) + pad, axis=1)
    tq = TQ_C
    nq = t // tq
    rows = C_GROUP * tq
    gw = C_GROUP * C_HD
    return pl.pallas_call(
        functools.partial(_flash_body, nch=nch, tail_keys=tail_keys),
        grid=(bsz, C_KV, nq + 1),
        in_specs=[pl.BlockSpec((None, tq, gw), lambda b, h, i: (b, jnp.minimum(i, nq - 1), h)),
                  pl.BlockSpec((None, n, C_HD), lambda b, h, i: (b, 0, h)),
                  pl.BlockSpec((None, n, C_HD), lambda b, h, i: (b, 0, h))],
        out_specs=pl.BlockSpec((None, tq, gw), lambda b, h, i: (b, jnp.maximum(i - 1, 0), h)),
        out_shape=jax.ShapeDtypeStruct((bsz, t, C_QW), BF16),
        scratch_shapes=[pltpu.VMEM((rows, C_HD), BF16),
                        pltpu.VMEM((2, nch, rows, KC_C), F32),
                        pltpu.VMEM((rows, KC_C), BF16),
                        pltpu.VMEM((rows, KC_C), BF16),
                        pltpu.VMEM((2, rows, LANE), F32),
                        pltpu.VMEM((rows, LANE), F32),
                        pltpu.VMEM((rows, LANE), F32),
                        pltpu.VMEM((rows, C_HD), F32)],
        compiler_params=_cparams(("parallel", "parallel", "arbitrary"), 48),
        name="flash",
    )(q, k, v)


def _c_out_body(x_ref, mod_ref, o_ref, w_ref, out_ref):
    mod = mod_ref[...]
    y = _dot(o_ref[...], w_ref[...])
    out_ref[...] = _res_ln(x_ref[...], y, mod[2:3], mod[3:4], mod[4:5])


def _c_out(x, mod, o, w_out, l):
    bsz, t, d = x.shape
    tm = _token_tile(t)
    per_batch = mod.shape[0] > 1
    return pl.pallas_call(
        _c_out_body,
        grid=(bsz, t // tm),
        in_specs=[pl.BlockSpec((None, tm, d), lambda b, i: (b, i, 0)),
                  pl.BlockSpec((None, 8, d), lambda b, i: (b if per_batch else 0, 0, 0)),
                  pl.BlockSpec((None, tm, C_QW), lambda b, i: (b, i, 0)),
                  pl.BlockSpec((None, C_QW, d), lambda b, i: (l, 0, 0), pipeline_mode=pl.Buffered(1))],
        out_specs=pl.BlockSpec((None, tm, d), lambda b, i: (b, i, 0)),
        out_shape=jax.ShapeDtypeStruct(x.shape, F32),
        compiler_params=_cparams(("parallel", "parallel"), 32),
        name="c_out",
    )(x, mod, o, w_out)


def _rope_half_tile(x, cos2, sin2, lane_lo):
    half = A_HD // 2
    partner = jnp.where(lane_lo, pltpu.roll(x, LANE - half, 1), pltpu.roll(x, half, 1))
    return x * cos2 + partner * sin2


def _gate_act(raw, is_beta, a_log, dt_bias):
    sp_in = raw + dt_bias
    softplus = jnp.maximum(sp_in, 0.0) + jnp.log(1.0 + jnp.exp(-jnp.abs(sp_in)))
    return jnp.where(is_beta, jax.nn.sigmoid(raw), -jnp.exp(a_log) * softplus)


def _ab_prep_body(*refs, rope, tm):
    if rope:
        (xp_ref, x_ref, xn_ref, mod_ref, w_ref, wgt_ref, cw_ref, gp_ref, gpt_ref, cos_ref, sin_ref,
         aq_ref, ak_ref, av_ref, bq_ref, bk_ref, bv_ref, z_ref, gc_ref, gr_ref, cbuf) = refs
    else:
        (xp_ref, x_ref, xn_ref, mod_ref, w_ref, wgt_ref, cw_ref, gp_ref, gpt_ref,
         aq_ref, ak_ref, av_ref, bq_ref, bk_ref, bv_ref, z_ref, gc_ref, gr_ref, cbuf) = refs
    i = pl.program_id(1)
    nt = pl.num_programs(1)
    mod = mod_ref[...]
    scale1, shift = 1.0 + mod[1:2], mod[0:1]
    h = (x_ref[...] * scale1 + shift).astype(BF16)
    proj = _dot(h, w_ref[...])

    if rope:
        cos2, sin2 = cos_ref[...], sin_ref[...]
        lane_lo = (lax.broadcasted_iota(jnp.int32, (tm, LANE), 1) % A_HD) < (A_HD // 2)
    for j in range(A_Q // LANE):
        a = proj[:, j * LANE:(j + 1) * LANE]
        if rope:
            a = _rope_half_tile(a, cos2, sin2, lane_lo)
        aq_ref[:, j * LANE:(j + 1) * LANE] = (a * (A_HD ** -0.5 * LOG2E)).astype(BF16)
    a = proj[:, OFF_AK:OFF_AK + LANE]
    if rope:
        a = _rope_half_tile(a, cos2, sin2, lane_lo)
    ak_ref[...] = a.astype(BF16)
    av_ref[...] = proj[:, OFF_AV:OFF_AV + LANE].astype(BF16)
    z_ref[...] = proj[:, OFF_BZ:OFF_BZ + B_VW]

    gp = gp_ref[...]
    lane = lax.broadcasted_iota(jnp.int32, (1, LANE), 1)
    gc_ref[...] = _gate_act(proj[:, OFF_GATE:OFF_GATE + LANE], lane < 2 * B_HEADS, gp[0:1], gp[1:2])
    gpt = gpt_ref[...]
    row = lax.broadcasted_iota(jnp.int32, (N_GATE, 1), 0)
    gr_ref[...] = _gate_act(_dot_nt(wgt_ref[...], h), row < 2 * B_HEADS, gpt[:, 0:1], gpt[:, 1:2])

    w_b = w_ref[:, OFF_BQKV:OFF_BQKV + B_QKV]
    hp = (xp_ref[...] * scale1 + shift).astype(BF16)
    hn = (xn_ref[...] * scale1 + shift).astype(BF16)
    cbuf[0:HALO, :] = _dot(hp, w_b) * jnp.where(i > 0, 1.0, 0.0)
    cbuf[HALO:HALO + tm, :] = proj[:, OFF_BQKV:OFF_BQKV + B_QKV]
    cbuf[HALO + tm:, :] = _dot(hn, w_b) * jnp.where(i < nt - 1, 1.0, 0.0)
    cw = cw_ref[...]
    pad = CONV_K // 2
    for part, dst in enumerate((bq_ref, bk_ref, bv_ref)):
        for hd in range(B_HEADS):
            c0 = part * B_QK + hd * B_DK
            acc = None
            for j in range(CONV_K):
                term = cbuf[HALO - pad + j:HALO - pad + j + tm, c0:c0 + B_DK] * cw[j:j + 1, c0:c0 + B_DK]
                acc = term if acc is None else acc + term
            acc = _silu(acc)
            if part < 2:
                acc = acc * lax.rsqrt(jnp.sum(acc * acc, axis=-1, keepdims=True) + NORM_EPS)
            if part == 0:
                acc = acc * (B_DK ** -0.5)
            dst[:, hd * B_DK:(hd + 1) * B_DK] = acc


def _ab_prep(x, mod, w_pad, w_gate_t, conv_w, gate_p, gate_pt, l, tables):
    bsz, t, d = x.shape
    tm = _token_tile(t)
    nh = tm // HALO
    nblk = t // HALO
    per_batch = mod.shape[0] > 1
    rope = tables is not None
    const = dict(pipeline_mode=pl.Buffered(1))
    in_specs = [pl.BlockSpec((None, HALO, d), lambda b, i: (b, jnp.maximum(i * nh - 1, 0), 0)),
                pl.BlockSpec((None, tm, d), lambda b, i: (b, i, 0)),
                pl.BlockSpec((None, HALO, d), lambda b, i: (b, jnp.minimum((i + 1) * nh, nblk - 1), 0)),
                pl.BlockSpec((None, 8, d), lambda b, i: (b if per_batch else 0, 0, 0)),
                pl.BlockSpec((None, d, AB_IN_PAD), lambda b, i: (l, 0, 0), **const),
                pl.BlockSpec((None, N_GATE, d), lambda b, i: (l, 0, 0), **const),
                pl.BlockSpec((None, 8, B_QKV), lambda b, i: (l, 0, 0)),
                pl.BlockSpec((None, 8, LANE), lambda b, i: (l, 0, 0)),
                pl.BlockSpec((None, N_GATE, LANE), lambda b, i: (l, 0, 0))]
    args = [x, x, x, mod, w_pad, w_gate_t, conv_w, gate_p, gate_pt]
    if rope:
        in_specs += [pl.BlockSpec((tm, LANE), lambda b, i: (i, 0))] * 2
        args += list(tables)
    tok = lambda w: pl.BlockSpec((None, tm, w), lambda b, i: (b, i, 0))
    shp = lambda w, dt: jax.ShapeDtypeStruct((bsz, t, w), dt)
    return pl.pallas_call(
        functools.partial(_ab_prep_body, rope=rope, tm=tm),
        grid=(bsz, t // tm),
        in_specs=in_specs,
        out_specs=[tok(A_Q), tok(A_KVW), tok(A_KVW), tok(B_QK), tok(B_QK), tok(B_VW), tok(B_VW), tok(LANE),
                   pl.BlockSpec((None, N_GATE, tm), lambda b, i: (b, 0, i))],
        out_shape=[shp(A_Q, BF16), shp(A_KVW, BF16), shp(A_KVW, BF16), shp(B_QK, F32), shp(B_QK, F32),
                   shp(B_VW, F32), shp(B_VW, F32), shp(LANE, F32),
                   jax.ShapeDtypeStruct((bsz, N_GATE, t), F32)],
        scratch_shapes=[pltpu.VMEM((tm + 2 * HALO, B_QKV), F32)],
        compiler_params=_cparams(("parallel", "parallel"), 52),
        name="ab_prep",
    )(*args)


A_GROUP = A_HEADS // A_KV


WIN_TILE = 512
WIN_GROUP = 1
A_HEAD_PERM = tuple(h for t in range(A_GROUP) for h in (t, A_GROUP + t))


def _win_body(*refs, local, nqb):
    if local:
        q_ref, kp_ref, kc_ref, kn_ref, vp_ref, vc_ref, vn_ref, kx_ref, vx_ref, sink_ref, o_ref = refs
    else:
        q_ref, kx_ref, vx_ref, sink_ref, o_ref = refs
    lane = lax.broadcasted_iota(jnp.int32, (1, LANE), 1)
    kv_lanes = (lane < A_HD, lane >= A_HD)
    kx = kx_ref[...]
    vx = vx_ref[...]
    sink = sink_ref[...] * LOG2E
    if local:
        i = pl.program_id(1)
        nsteps = pl.num_programs(1)
        kcat = jnp.concatenate([kp_ref[...], kc_ref[...], kn_ref[...]], axis=0)
        vcat = jnp.concatenate([vp_ref[...], vc_ref[...], vn_ref[...]], axis=0)
        r = lax.broadcasted_iota(jnp.int32, (BLOCK, 3 * BLOCK), 0)
        c = lax.broadcasted_iota(jnp.int32, (BLOCK, 3 * BLOCK), 1)
        band = (c >= r) & (c <= r + 2 * WINDOW)
    zero = jnp.zeros((), BF16)
    sks = [jnp.concatenate([jnp.broadcast_to(sink[kv * A_GROUP + g:kv * A_GROUP + g + 1, 0:1], (BLOCK, 1))
                            for g in range(A_GROUP)], axis=0) for kv in range(A_KV)]
    def logits(qb0):
        units = []
        for qb in range(qb0, min(qb0 + WIN_GROUP, nqb)):
            r0 = qb * BLOCK
            tiles = [q_ref[r0:r0 + BLOCK, t * LANE:(t + 1) * LANE] for t in range(A_GROUP)]
            for kv in range(A_KV):
                qs = jnp.concatenate([jnp.where(kv_lanes[kv], tl, zero) for tl in tiles], axis=0)
                units.append(dict(qb=qb, kv=kv, r0=r0, qs=qs))
        for un in units:
            un["s_x"] = _dot_nt(un["qs"], kx)
            if local:
                un["s_l"] = _dot_nt(un["qs"], kcat[un["r0"]:un["r0"] + 3 * BLOCK])
        return units

    def finish(units):
        for un in units:
            m = jnp.maximum(jnp.max(un["s_x"], axis=-1, keepdims=True), sks[un["kv"]])
            if local:
                valid = band
                if un["qb"] == 0:
                    valid = valid & ((c >= BLOCK) | (i > 0))
                if un["qb"] == nqb - 1:
                    valid = valid & ((c < 2 * BLOCK) | (i < nsteps - 1))
                un["s_l"] = jnp.concatenate(
                    [jnp.where(valid, un["s_l"][g * BLOCK:(g + 1) * BLOCK], NEG) for g in range(A_GROUP)], axis=0)
                m = jnp.maximum(m, jnp.max(un["s_l"], axis=-1, keepdims=True))
            un["m"] = m
        for un in units:
            p_x = jnp.exp2(un["s_x"] - un["m"])
            un["den"] = jnp.sum(p_x, axis=-1, keepdims=True) + jnp.exp2(sks[un["kv"]] - un["m"])
            un["p_x"] = p_x.astype(BF16)
            if local:
                p_l = jnp.exp2(un["s_l"] - un["m"])
                un["den"] = un["den"] + jnp.sum(p_l, axis=-1, keepdims=True)
                un["p_l"] = p_l.astype(BF16)
        for un in units:
            o = _dot(un["p_x"], vx)
            if local:
                o = o + _dot(un["p_l"], vcat[un["r0"]:un["r0"] + 3 * BLOCK])
            un["o"] = o / un["den"]
        for u0 in range(0, len(units), A_KV):
            r0 = units[u0]["r0"]
            for t in range(A_GROUP):
                o_ref[r0:r0 + BLOCK, t * LANE:(t + 1) * LANE] = jnp.where(
                    kv_lanes[0], units[u0]["o"][t * BLOCK:(t + 1) * BLOCK],
                    units[u0 + 1]["o"][t * BLOCK:(t + 1) * BLOCK]).astype(BF16)

    starts = list(range(0, nqb, WIN_GROUP))
    pending = logits(starts[0])
    for nxt in starts[1:] + [None]:
        ahead = logits(nxt) if nxt is not None else None
        finish(pending)
        pending = ahead


def _win_attn(q, kx, vx, sink, l, k=None, v=None):
    bsz, t, _ = q.shape
    lc = kx.shape[1]
    local = k is not None
    tq = min(WIN_TILE, t)
    nqb = tq // BLOCK
    nb = t // BLOCK
    edge = lambda f: pl.BlockSpec((None, BLOCK, A_KVW), f)
    in_specs = [pl.BlockSpec((None, tq, A_Q), lambda b, i: (b, i, 0))]
    args = [q]
    if local:
        prv = lambda b, i: (b, jnp.maximum(i * nqb - 1, 0), 0)
        nxt = lambda b, i: (b, jnp.minimum((i + 1) * nqb, nb - 1), 0)
        cur = pl.BlockSpec((None, tq, A_KVW), lambda b, i: (b, i, 0))
        in_specs += [edge(prv), cur, edge(nxt), edge(prv), cur, edge(nxt)]
        args += [k, k, k, v, v, v]
    in_specs += [pl.BlockSpec((None, lc, A_KVW), lambda b, i: (b, 0, 0)),
                 pl.BlockSpec((None, lc, A_KVW), lambda b, i: (b, 0, 0)),
                 pl.BlockSpec((None, A_HEADS, LANE), lambda b, i: (l, 0, 0))]
    args += [kx, vx, sink]
    return pl.pallas_call(
        functools.partial(_win_body, local=local, nqb=nqb),
        grid=(bsz, t // tq),
        in_specs=in_specs,
        out_specs=pl.BlockSpec((None, tq, A_Q), lambda b, i: (b, i, 0)),
        out_shape=jax.ShapeDtypeStruct((bsz, t, A_Q), BF16),
        compiler_params=_cparams(("parallel", "parallel"), 40),
        name="win_attn",
    )(*args)


DELTA_BLOCK = 2 * CHUNK
N_CHAIN = 2 * B_HEADS


def _split_bf16(a):
    hi = a.astype(BF16)
    return hi, (a - hi.astype(F32)).astype(BF16)


def _delta_intra(units, eye):
    for un in units:
        decay = jnp.exp(jnp.where(un["incl"], un["gcc"] - un["gcr"], NEG))
        kb = un["k"] * un["beta"]
        a = _dot_nt(jnp.concatenate([kb, un["q"]], axis=0).astype(BF16), un["k"].astype(BF16))
        un["p"] = jnp.where(un["strict"], a[:CHUNK] * decay, 0.0) * -1.0
        un["attn"] = (a[CHUNK:] * decay).astype(BF16)
        un["t"] = eye + un["p"]
        eg = jnp.exp(un["gcc"])
        un["rhs"] = jnp.concatenate([un["v"] * un["beta"], kb * eg], axis=1).astype(BF16)
        un["qd"] = un["q"] * eg
        un["kt_t"] = (un["k"] * jnp.exp(un["gtot"] - un["gcc"])).T.astype(BF16)
        un["dec"] = jnp.exp(un["gtot"])
    for _ in range(CHUNK.bit_length() - 2):
        for un in units:
            pb = un["p"].astype(BF16)
            un["p"] = _dot(pb, pb)
        for un in units:
            un["t"] = un["t"] + _dot(un["t"].astype(BF16), un["p"].astype(BF16))
    for un in units:
        uw = _dot(un["t"].astype(BF16), un["rhs"])
        un["u"] = uw[:, :B_DV]
        un["wq"] = jnp.concatenate([uw[:, B_DV:], un["qd"]], axis=0).astype(BF16)


def _delta_body(qf_ref, kf_ref, vf_ref, gcf_ref, grf_ref, qb_ref, kb_ref, vb_ref, gcb_ref, grb_ref, s0_ref,
                of_ref, ob_ref, s_ref):
    @pl.when(pl.program_id(1) == 0)
    def _():
        s_ref[...] = s0_ref[...]

    li = lax.broadcasted_iota(jnp.int32, (CHUNK, CHUNK), 0)
    lj = lax.broadcasted_iota(jnp.int32, (CHUNK, CHUNK), 1)
    eye = jnp.where(li == lj, 1.0, 0.0)
    tri_l = jnp.where(li >= lj, 1.0, 0.0).astype(BF16)
    tri_u = jnp.where(li <= lj, 1.0, 0.0).astype(BF16)

    dirs = ((qf_ref, kf_ref, vf_ref, gcf_ref, grf_ref, of_ref, (0, 1), tri_l, tri_u, li >= lj, li > lj, CHUNK - 1),
            (qb_ref, kb_ref, vb_ref, gcb_ref, grb_ref, ob_ref, (1, 0), tri_u, tri_l, li <= lj, li < lj, 0))
    units = {}
    for d, (q_ref, k_ref, v_ref, gc_ref, gr_ref, o_ref, order, tri_c, tri_r, incl, strict, last) in enumerate(dirs):
        for c in order:
            r0 = c * CHUNK
            gcol = gc_ref[r0:r0 + CHUNK, :]
            ghi, glo = _split_bf16(gcol)
            cum_c = _dot(tri_c, ghi) + _dot(tri_c, glo)
            grow = gr_ref[:, r0:r0 + CHUNK]
            rhi, rlo = _split_bf16(grow)
            cum_r = _dot(rhi, tri_r) + _dot(rlo, tri_r)
            for hd in range(B_HEADS):
                col = d * B_HEADS + hd
                gl = 2 * B_HEADS + col
                lo, hi = hd * B_DK, (hd + 1) * B_DK
                units[(d, c, hd)] = dict(
                    q=q_ref[r0:r0 + CHUNK, lo:hi], k=k_ref[r0:r0 + CHUNK, lo:hi], v=v_ref[r0:r0 + CHUNK, lo:hi],
                    beta=gcol[:, col:col + 1], gcc=cum_c[:, gl:gl + 1], gcr=cum_r[gl:gl + 1, :],
                    gtot=cum_c[last:last + 1, gl:gl + 1], incl=incl, strict=strict)
    _delta_intra(list(units.values()), eye)

    chains = [(d, hd) for d in range(2) for hd in range(B_HEADS)]
    state = [s_ref[d * B_HEADS + hd] for d, hd in chains]
    for step in range(2):
        cur = [units[(d, dirs[d][6][step], hd)] for d, hd in chains]
        ws = [_dot(un["wq"], s.astype(BF16)) for un, s in zip(cur, state)]
        v_new = [(un["u"] - w[:CHUNK]).astype(BF16) for un, w in zip(cur, ws)]
        outs = [w[CHUNK:] + _dot(un["attn"], vn) for un, w, vn in zip(cur, ws, v_new)]
        state = [s * un["dec"] + _dot(un["kt_t"], vn) for un, s, vn in zip(cur, state, v_new)]
        for (d, hd), o in zip(chains, outs):
            r0 = dirs[d][6][step] * CHUNK
            dirs[d][5][r0:r0 + CHUNK, hd * B_DV:(hd + 1) * B_DV] = o
    for (d, hd), s in zip(chains, state):
        s_ref[d * B_HEADS + hd] = s


def _delta(bq, bk, bv, gcol, grow, s0):
    bsz, t, _ = bq.shape
    ns = t // DELTA_BLOCK
    fwd = lambda w: pl.BlockSpec((None, DELTA_BLOCK, w), lambda b, s: (b, s, 0))
    bwd = lambda w: pl.BlockSpec((None, DELTA_BLOCK, w), lambda b, s: (b, ns - 1 - s, 0))
    st = pl.BlockSpec((None, N_CHAIN, B_DK, B_DV), lambda b, s: (b, 0, 0, 0))
    return pl.pallas_call(
        _delta_body,
        grid=(bsz, ns),
        in_specs=[fwd(B_QK), fwd(B_QK), fwd(B_VW), fwd(LANE),
                  pl.BlockSpec((None, N_GATE, DELTA_BLOCK), lambda b, s: (b, 0, s)),
                  bwd(B_QK), bwd(B_QK), bwd(B_VW), bwd(LANE),
                  pl.BlockSpec((None, N_GATE, DELTA_BLOCK), lambda b, s: (b, 0, ns - 1 - s)),
                  st],
        out_specs=[fwd(B_VW), bwd(B_VW), st],
        out_shape=[jax.ShapeDtypeStruct((bsz, t, B_VW), F32), jax.ShapeDtypeStruct((bsz, t, B_VW), F32),
                   jax.ShapeDtypeStruct((bsz, N_CHAIN, B_DK, B_DV), F32)],
        compiler_params=_cparams(("parallel", "arbitrary"), 32),
        name="delta",
    )(bq, bk, bv, gcol, grow, bq, bk, bv, gcol, grow, s0)


def _ab_out_body(x_ref, mod_ref, oa_ref, of_ref, ob_ref, z_ref, gn_ref, w_ref, out_ref):
    mod = mod_ref[...]
    gn = gn_ref[...]
    y = _dot(oa_ref[...], w_ref[0:A_Q, :])
    for hd in range(B_HEADS):
        lo, hi = hd * B_DV, (hd + 1) * B_DV
        o = of_ref[:, lo:hi] + ob_ref[:, lo:hi]
        o = o * lax.rsqrt(jnp.mean(o * o, axis=-1, keepdims=True) + NORM_EPS) * gn
        o = (o * _silu(z_ref[:, lo:hi])).astype(BF16)
        y = y + _dot(o, w_ref[A_Q + lo:A_Q + hi, :])
    out_ref[...] = _res_ln(x_ref[...], y, mod[2:3], mod[3:4], mod[4:5])


def _ab_out(x, mod, oa, of, ob, z, gnorm, w_out, l):
    bsz, t, d = x.shape
    tm = _token_tile(t)
    per_batch = mod.shape[0] > 1
    tok = lambda w: pl.BlockSpec((None, tm, w), lambda b, i: (b, i, 0))
    return pl.pallas_call(
        _ab_out_body,
        grid=(bsz, t // tm),
        in_specs=[tok(d),
                  pl.BlockSpec((None, 8, d), lambda b, i: (b if per_batch else 0, 0, 0)),
                  tok(A_Q), tok(B_VW), tok(B_VW), tok(B_VW),
                  pl.BlockSpec((None, 1, B_DV), lambda b, i: (l, 0, 0)),
                  pl.BlockSpec((None, A_Q + B_VW, d), lambda b, i: (l, 0, 0), pipeline_mode=pl.Buffered(1))],
        out_specs=tok(d),
        out_shape=jax.ShapeDtypeStruct(x.shape, F32),
        compiler_params=_cparams(("parallel", "parallel"), 40),
        name="ab_out",
    )(x, mod, oa, of, ob, z, gnorm, w_out)


def _rope_tables(rows, head_dim):
    n_freq = head_dim // 4
    inv = ROPE_THETA ** (-jnp.arange(n_freq, dtype=F32) / n_freq)
    r, col = jnp.meshgrid(jnp.arange(rows, dtype=F32), jnp.arange(GRID_W, dtype=F32), indexing='ij')
    r, col = r.reshape(-1), col.reshape(-1)
    ang = jnp.concatenate([r[:, None] * inv, col[:, None] * inv], axis=-1)
    cos, sin = jnp.cos(ang), jnp.sin(ang)
    cos2 = jnp.concatenate([cos, cos], axis=-1)
    sin2 = jnp.concatenate([-sin, sin], axis=-1)
    rep = LANE // head_dim
    return jnp.tile(cos2, (1, rep)), jnp.tile(sin2, (1, rep))


def _pad_rows(a, rows):
    return jnp.pad(a, ((0, 0), (0, rows - a.shape[1]), (0, 0)))


def kernel(x, c, ctx, c_ctx, ada_w, ada_b, ln_g, ln_b, ffn_w_gu, ffn_w_down, ab_w_in, ab_conv_w, ab_a_log,
           ab_dt_bias, ab_gnorm, ab_sink, ab_w_out, c_w_in, c_q_norm, c_k_norm, c_w_out):
    bsz, t, d = x.shape
    depth = ada_w.shape[0]
    n_even = ab_w_in.shape[0]
    rows = t // GRID_W
    tab_a = _rope_tables(rows, A_HD)
    tab_c = _rope_tables(rows, C_HD)

    wgu = ffn_w_gu.astype(BF16)
    wd = ffn_w_down.astype(BF16)
    head_cols = jnp.asarray([h * A_HD + j for h in A_HEAD_PERM for j in range(A_HD)], jnp.int32)
    ab_w_in_p = jnp.concatenate([jnp.take(ab_w_in[:, :, :A_Q], head_cols, axis=2), ab_w_in[:, :, A_Q:]], axis=2)
    ab_w_pad = jnp.pad(ab_w_in_p, ((0, 0), (0, 0), (0, AB_IN_PAD - AB_IN))).astype(BF16)
    ab_w_gate_t = jnp.swapaxes(ab_w_in[:, :, OFF_GATE:OFF_GATE + N_GATE], 1, 2).astype(BF16)
    ab_wo = jnp.concatenate([jnp.take(ab_w_out[:, :A_Q], head_cols, axis=1), ab_w_out[:, A_Q:]], axis=1).astype(BF16)
    c_wi = c_w_in.astype(BF16)
    c_wo = c_w_out.astype(BF16)
    conv_w = _pad_rows(ab_conv_w, 8)
    zeros8 = jnp.zeros((n_even, 2 * B_HEADS), F32)
    a_log16 = jnp.concatenate([zeros8, ab_a_log.reshape(n_even, 2 * B_HEADS)], axis=1)
    dtb16 = jnp.concatenate([zeros8, ab_dt_bias.reshape(n_even, 2 * B_HEADS)], axis=1)
    gate_p = _pad_rows(jnp.pad(jnp.stack([a_log16, dtb16], axis=1), ((0, 0), (0, 0), (0, LANE - N_GATE))), 8)
    gate_pt = jnp.pad(jnp.stack([a_log16, dtb16], axis=2), ((0, 0), (0, 0), (0, LANE - 2)))
    sink = jnp.broadcast_to(ab_sink[:, :, None], (n_even, A_HEADS, LANE))
    gnorm = ab_gnorm.reshape(n_even, 1, B_DV)
    qn = c_q_norm.reshape(-1, 1, C_HD)
    kn = c_k_norm.reshape(-1, 1, C_HD)

    nrow = ((bsz + 1 + 7) // 8) * 8
    cvec = jnp.pad(jnp.concatenate([c, c_ctx[None]], axis=0), ((0, nrow - bsz - 1), (0, 0)))
    m = _modulation(cvec, ada_w, ada_b).reshape(depth, nrow, N_MOD, d)

    def mod_rows(l, s):
        ln = jnp.stack([ln_g[l, s], ln_b[l, s]], axis=0)
        rows_l = jnp.concatenate([m[l, :bsz, 3 * s:3 * s + 3], jnp.broadcast_to(ln, (bsz, 2, d)),
                                  jnp.zeros((bsz, 3, d), F32)], axis=1)
        rows_c = jnp.concatenate([m[l, bsz:bsz + 1, 3 * s:3 * s + 3], ln[None],
                                  jnp.zeros((1, 3, d), F32)], axis=1)
        return rows_l, rows_c

    xl, xc = x, ctx
    for l in range(depth):
        ctx_out = l < depth - 1
        i = l // 2
        m0l, m0c = mod_rows(l, 0)
        m1l, m1c = mod_rows(l, 1)
        m2l, m2c = mod_rows(l, 2)
        xl = _ffn(xl, m0l, wgu, wd, l, 0)
        xc = _ffn(xc, m0c, wgu, wd, l, 0)
        if l % 2 == 0:
            pc = _ab_prep(xc, m1c, ab_w_pad, ab_w_gate_t, conv_w, gate_p, gate_pt, i, None)
            pl_ = _ab_prep(xl, m1l, ab_w_pad, ab_w_gate_t, conv_w, gate_p, gate_pt, i, tab_a)
            aqc, akc, avc, bqc, bkc, bvc, zc, gcc, grc = pc
            aql, akl, avl, bql, bkl, bvl, zl, gcl, grl = pl_
            ol_a = _win_attn(aql, akc, avc, sink, i, akl, avl)
            s0 = jnp.zeros((bsz, N_CHAIN, B_DK, B_DV), F32)
            oc_f, oc_b, s_ctx = _delta(bqc, bkc, bvc, gcc, grc, s0)
            ol_f, ol_b, _ = _delta(bql, bkl, bvl, gcl, grl, s_ctx)
            xl = _ab_out(xl, m1l, ol_a, ol_f, ol_b, zl, gnorm, ab_wo, i)
            if ctx_out:
                oc_a = _win_attn(aqc, akc, avc, sink, i)
                xc = _ab_out(xc, m1c, oc_a, oc_f, oc_b, zc, gnorm, ab_wo, i)
        else:
            qc, kc, vc = _c_prep(xc, m1c, c_wi, i, qn[i], kn[i], None)
            ql, kl, vl = _c_prep(xl, m1l, c_wi, i, qn[i], kn[i], tab_c)
            ol = _flash(ql, [kc, kl], [vc, vl])
            xl = _c_out(xl, m1l, ol, c_wo, i)
            if ctx_out:
                oc = _flash(qc, [kc], [vc])
                xc = _c_out(xc, m1c, oc, c_wo, i)
        xl = _ffn(xl, m2l, wgu, wd, l, 2 - 1)
        if ctx_out:
            xc = _ffn(xc, m2c, wgu, wd, l, 1)
    return xl
```

```python
import functools

import jax
import jax.numpy as jnp
from jax import lax
from jax.experimental import pallas as pl
from jax.experimental.pallas import tpu as pltpu

F32 = jnp.float32
BF16 = jnp.bfloat16

D_MODEL = 1024
DEPTH = 4
GRID_W = 64
D_FF = 2816
N_SUB = 3
N_MOD = 3 * N_SUB
FFN_HALF = 0.5
NORM_EPS = 1e-6
ROPE_THETA = 10000.0
DEEP_ALPHA = (2 * DEPTH) ** 0.25

A_HEADS, A_KV, A_HD = 8, 2, 64
WINDOW = 128
BLOCK = 128
B_HEADS, B_DK, B_DV = 4, 128, 128
CONV_K = 5
CHUNK = 64
C_HEADS, C_KV, C_HD = 8, 2, 128

A_Q = A_HEADS * A_HD
A_KVW = A_KV * A_HD
B_QK = B_HEADS * B_DK
B_VW = B_HEADS * B_DV
B_QKV = 2 * B_QK + B_VW
AB_IN = A_Q + 2 * A_KVW + B_QKV + B_VW + 4 * B_HEADS
N_GATE = 4 * B_HEADS
LANE = 128
AB_IN_PAD = AB_IN - N_GATE + LANE
OFF_AK = A_Q
OFF_AV = A_Q + A_KVW
OFF_BQKV = A_Q + 2 * A_KVW
OFF_BZ = OFF_BQKV + B_QKV
OFF_GATE = OFF_BZ + B_VW
C_QW = C_HEADS * C_HD
C_KW = C_KV * C_HD
C_IN = C_QW + 2 * C_KW
HALO = 8
NEG = -1e30
MIB = 1024 * 1024


def _cparams(sem, vmem_mib):
    return pltpu.CompilerParams(dimension_semantics=sem, vmem_limit_bytes=vmem_mib * MIB)


def _dot(a, b):
    return jnp.dot(a, b, preferred_element_type=F32)


def _dot_nt(a, b):
    return lax.dot_general(a, b, (((1,), (1,)), ((), ())), preferred_element_type=F32)


def _silu(x):
    return x * jax.nn.sigmoid(x)


def _res_ln(x, y, gate, g, b):
    z = DEEP_ALPHA * x + gate * y
    mu = jnp.mean(z, axis=-1, keepdims=True)
    zc = z - mu
    var = jnp.mean(zc * zc, axis=-1, keepdims=True)
    return zc * lax.rsqrt(var + NORM_EPS) * g + b


def _token_tile(t):
    return 512 if t % 512 == 0 else 256


def _mod_body(c_ref, w_ref, b_ref, o_ref):
    c = c_ref[...]
    o_ref[...] = jnp.dot(_silu(c), w_ref[...], preferred_element_type=F32,
                         precision=lax.Precision.HIGHEST) + b_ref[...]


def _modulation(cvec, ada_w, ada_b):
    depth, d, n = ada_w.shape
    rows = cvec.shape[0]
    tn = 2304
    return pl.pallas_call(
        _mod_body,
        grid=(depth, n // tn),
        in_specs=[pl.BlockSpec((rows, d), lambda l, j: (0, 0)),
                  pl.BlockSpec((None, d, tn), lambda l, j: (l, 0, j)),
                  pl.BlockSpec((None, 1, tn), lambda l, j: (l, 0, j))],
        out_specs=pl.BlockSpec((None, rows, tn), lambda l, j: (l, 0, j)),
        out_shape=jax.ShapeDtypeStruct((depth, rows, n), F32),
        compiler_params=_cparams(("parallel", "parallel"), 40),
        name="mod",
    )(cvec, ada_w, ada_b.reshape(depth, 1, n))


FFN_CHUNKS = 11


def _ffn_body(x_ref, mod_ref, wgu_ref, wd_ref, o_ref):
    x = x_ref[...]
    mod = mod_ref[...]
    h = (x * (1.0 + mod[1:2]) + mod[0:1]).astype(BF16)
    fc = D_FF // FFN_CHUNKS
    y = None
    for j in range(FFN_CHUNKS):
        gt = _dot(h, wgu_ref[:, j * fc:(j + 1) * fc])
        up = _dot(h, wgu_ref[:, D_FF + j * fc:D_FF + (j + 1) * fc])
        act = (_silu(gt) * up).astype(BF16)
        part = _dot(act, wd_ref[j * fc:(j + 1) * fc, :])
        y = part if y is None else y + part
    o_ref[...] = _res_ln(x, y, FFN_HALF * mod[2:3], mod[3:4], mod[4:5])


def _ffn(x, mod, wgu, wd, l, s):
    bsz, t, d = x.shape
    tm = _token_tile(t)
    per_batch = mod.shape[0] > 1
    return pl.pallas_call(
        _ffn_body,
        grid=(bsz, t // tm),
        in_specs=[pl.BlockSpec((None, tm, d), lambda b, i: (b, i, 0)),
                  pl.BlockSpec((None, 8, d), lambda b, i: (b if per_batch else 0, 0, 0)),
                  pl.BlockSpec((None, None, d, 2 * D_FF), lambda b, i: (l, s, 0, 0),
                               pipeline_mode=pl.Buffered(1)),
                  pl.BlockSpec((None, None, D_FF, d), lambda b, i: (l, s, 0, 0),
                               pipeline_mode=pl.Buffered(1))],
        out_specs=pl.BlockSpec((None, tm, d), lambda b, i: (b, i, 0)),
        out_shape=jax.ShapeDtypeStruct(x.shape, F32),
        compiler_params=_cparams(("parallel", "parallel"), 52),
        name="ffn",
    )(x, mod, wgu, wd)


def _rope_full(x, cos2, sin2):
    return x * cos2 + pltpu.roll(x, C_HD // 2, 1) * sin2


def _c_prep_body(*refs, rope):
    if rope:
        x_ref, mod_ref, w_ref, qn_ref, kn_ref, cos_ref, sin_ref, q_ref, k_ref, v_ref = refs
        cos2, sin2 = cos_ref[...], sin_ref[...]
    else:
        x_ref, mod_ref, w_ref, qn_ref, kn_ref, q_ref, k_ref, v_ref = refs
    x = x_ref[...]
    mod = mod_ref[...]
    h = (x * (1.0 + mod[1:2]) + mod[0:1]).astype(BF16)
    qkv = _dot(h, w_ref[...])
    qn = qn_ref[...] * (C_HD ** -0.5 * LOG2E)
    kn = kn_ref[...]

    def head(col, gain):
        a = qkv[:, col:col + C_HD]
        a = a * lax.rsqrt(jnp.mean(a * a, axis=-1, keepdims=True) + NORM_EPS) * gain
        if rope:
            a = _rope_full(a, cos2, sin2)
        return a.astype(BF16)

    for j in range(C_HEADS):
        q_ref[:, j * C_HD:(j + 1) * C_HD] = head(j * C_HD, qn)
    for j in range(C_KV):
        k_ref[:, j * C_HD:(j + 1) * C_HD] = head(C_QW + j * C_HD, kn)
    v_ref[...] = qkv[:, C_QW + C_KW:].astype(BF16)


def _c_prep(x, mod, w_in, l, qn, kn, tables):
    bsz, t, d = x.shape
    tm = _token_tile(t)
    per_batch = mod.shape[0] > 1
    rope = tables is not None
    in_specs = [pl.BlockSpec((None, tm, d), lambda b, i: (b, i, 0)),
                pl.BlockSpec((None, 8, d), lambda b, i: (b if per_batch else 0, 0, 0)),
                pl.BlockSpec((None, d, C_IN), lambda b, i: (l, 0, 0), pipeline_mode=pl.Buffered(1)),
                pl.BlockSpec((1, C_HD), lambda b, i: (0, 0)),
                pl.BlockSpec((1, C_HD), lambda b, i: (0, 0))]
    args = [x, mod, w_in, qn, kn]
    if rope:
        in_specs += [pl.BlockSpec((tm, C_HD), lambda b, i: (i, 0))] * 2
        args += list(tables)
    return pl.pallas_call(
        functools.partial(_c_prep_body, rope=rope),
        grid=(bsz, t // tm),
        in_specs=in_specs,
        out_specs=[pl.BlockSpec((None, tm, C_QW), lambda b, i: (b, i, 0)),
                   pl.BlockSpec((None, tm, C_KW), lambda b, i: (b, i, 0)),
                   pl.BlockSpec((None, tm, C_KW), lambda b, i: (b, i, 0))],
        out_shape=[jax.ShapeDtypeStruct((bsz, t, C_QW), BF16),
                   jax.ShapeDtypeStruct((bsz, t, C_KW), BF16),
                   jax.ShapeDtypeStruct((bsz, t, C_KW), BF16)],
        compiler_params=_cparams(("parallel", "parallel"), 40),
        name="c_prep",
    )(*args)


C_GROUP = C_HEADS // C_KV
TQ_C = 128
KC_C = 256
RB_C = 64
FLASH_UNROLL = 4
LOG2E = 1.4426950408889634


def _flash_body(q_ref, k_ref, v_ref, o_ref, q_scr, s_scr, p0_scr, p1_scr, m_scr, mb_scr, l_scr, acc_scr, *,
                nch, tail_keys):
    i = pl.program_id(2)
    slot_a = i % 2
    slot_b = 1 - slot_a
    rows = q_scr.shape[0]
    tq = q_ref.shape[0]

    @pl.when(i == 0)
    def _():
        s_scr[1] = jnp.zeros(s_scr.shape[1:], F32)
        m_scr[1] = jnp.zeros(m_scr.shape[1:], F32)

    for g in range(C_GROUP):
        q_scr[g * tq:(g + 1) * tq, :] = q_ref[:, g * C_HD:(g + 1) * C_HD]
    m_cur = jnp.max(m_scr[slot_b], axis=-1, keepdims=True)
    mb_scr[...] = jnp.broadcast_to(m_cur, (rows, LANE))
    m_scr[slot_a] = jnp.full((rows, LANE), NEG, F32)
    l_scr[...] = jnp.zeros((rows, LANE), F32)
    acc_scr[...] = jnp.zeros((rows, C_HD), F32)

    def probs(j, p_scr):
        for r0 in range(0, rows, RB_C):
            mb = mb_scr[r0:r0 + RB_C, :]
            lt = l_scr[r0:r0 + RB_C, :]
            for c0 in range(0, KC_C, LANE):
                p = jnp.exp2(s_scr[slot_b, j, r0:r0 + RB_C, c0:c0 + LANE] - mb)
                lt = lt + p
                p_scr[r0:r0 + RB_C, c0:c0 + LANE] = p.astype(BF16)
            l_scr[r0:r0 + RB_C, :] = lt

    def pv(j, p_scr):
        k0 = pl.multiple_of(j * KC_C, KC_C)
        acc_scr[...] += _dot(p_scr[...], v_ref[pl.ds(k0, KC_C), :])

    def logits(j, n_keys=KC_C):
        k0 = pl.multiple_of(j * KC_C, KC_C)
        s = _dot_nt(q_scr[...], k_ref[pl.ds(k0, KC_C), :])
        if n_keys < KC_C:
            s = jnp.where(lax.broadcasted_iota(jnp.int32, (1, KC_C), 1) < n_keys, s, NEG)
        s_scr[slot_a, j] = s
        mt = s[:, 0:LANE]
        for c0 in range(LANE, KC_C, LANE):
            mt = jnp.maximum(mt, s[:, c0:c0 + LANE])
        m_scr[slot_a] = jnp.maximum(m_scr[slot_a], mt)

    probs(0, p0_scr)

    def pair(t, carry):
        j = 2 * t
        logits(j)
        pv(j, p0_scr)
        probs(j + 1, p1_scr)
        logits(j + 1)
        pv(j + 1, p1_scr)
        probs(j + 2, p0_scr)
        return carry

    lax.fori_loop(0, (nch - 1) // 2, pair, 0, unroll=FLASH_UNROLL)
    pv(nch - 1, p0_scr)
    logits(nch - 1, tail_keys)

    o = acc_scr[...] / jnp.sum(l_scr[...], axis=-1, keepdims=True)
    for g in range(C_GROUP):
        o_ref[:, g * C_HD:(g + 1) * C_HD] = o[g * tq:(g + 1) * tq].astype(BF16)


def _flash(q, ks, vs):
    bsz, t, _ = q.shape
    n_keys = sum(a.shape[1] for a in ks)
    nch = -(-n_keys // KC_C)
    nch += 1 - nch % 2
    n = nch * KC_C
    tail_keys = n_keys - (nch - 1) * KC_C
    assert 0 < tail_keys <= KC_C
    pad = [jnp.zeros((bsz, n - n_keys, C_KW), BF16)] if n > n_keys else []
    k = jnp.concatenate(list(ks) + pad, axis=1)
    v = jnp.concatenate(list(vs) + pad, axis=1)
    tq = TQ_C
    nq = t // tq
    rows = C_GROUP * tq
    gw = C_GROUP * C_HD
    return pl.pallas_call(
        functools.partial(_flash_body, nch=nch, tail_keys=tail_keys),
        grid=(bsz, C_KV, nq + 1),
        in_specs=[pl.BlockSpec((None, tq, gw), lambda b, h, i: (b, jnp.minimum(i, nq - 1), h)),
                  pl.BlockSpec((None, n, C_HD), lambda b, h, i: (b, 0, h)),
                  pl.BlockSpec((None, n, C_HD), lambda b, h, i: (b, 0, h))],
        out_specs=pl.BlockSpec((None, tq, gw), lambda b, h, i: (b, jnp.maximum(i - 1, 0), h)),
        out_shape=jax.ShapeDtypeStruct((bsz, t, C_QW), BF16),
        scratch_shapes=[pltpu.VMEM((rows, C_HD), BF16),
                        pltpu.VMEM((2, nch, rows, KC_C), F32),
                        pltpu.VMEM((rows, KC_C), BF16),
                        pltpu.VMEM((rows, KC_C), BF16),
                        pltpu.VMEM((2, rows, LANE), F32),
                        pltpu.VMEM((rows, LANE), F32),
                        pltpu.VMEM((rows, LANE), F32),
                        pltpu.VMEM((rows, C_HD), F32)],
        compiler_params=_cparams(("parallel", "parallel", "arbitrary"), 48),
        name="flash",
    )(q, k, v)


def _c_out_body(x_ref, mod_ref, o_ref, w_ref, out_ref):
    mod = mod_ref[...]
    y = _dot(o_ref[...], w_ref[...])
    out_ref[...] = _res_ln(x_ref[...], y, mod[2:3], mod[3:4], mod[4:5])


def _c_out(x, mod, o, w_out, l):
    bsz, t, d = x.shape
    tm = _token_tile(t)
    per_batch = mod.shape[0] > 1
    return pl.pallas_call(
        _c_out_body,
        grid=(bsz, t // tm),
        in_specs=[pl.BlockSpec((None, tm, d), lambda b, i: (b, i, 0)),
                  pl.BlockSpec((None, 8, d), lambda b, i: (b if per_batch else 0, 0, 0)),
                  pl.BlockSpec((None, tm, C_QW), lambda b, i: (b, i, 0)),
                  pl.BlockSpec((None, C_QW, d), lambda b, i: (l, 0, 0), pipeline_mode=pl.Buffered(1))],
        out_specs=pl.BlockSpec((None, tm, d), lambda b, i: (b, i, 0)),
        out_shape=jax.ShapeDtypeStruct(x.shape, F32),
        compiler_params=_cparams(("parallel", "parallel"), 32),
        name="c_out",
    )(x, mod, o, w_out)


def _rope_half_tile(x, cos2, sin2, lane_lo):
    half = A_HD // 2
    partner = jnp.where(lane_lo, pltpu.roll(x, LANE - half, 1), pltpu.roll(x, half, 1))
    return x * cos2 + partner * sin2


def _gate_act(raw, is_beta, a_log, dt_bias):
    sp_in = raw + dt_bias
    softplus = jnp.maximum(sp_in, 0.0) + jnp.log(1.0 + jnp.exp(-jnp.abs(sp_in)))
    return jnp.where(is_beta, jax.nn.sigmoid(raw), -jnp.exp(a_log) * softplus)


def _ab_prep_body(*refs, rope, tm):
    if rope:
        (xp_ref, x_ref, xn_ref, mod_ref, w_ref, wgt_ref, cw_ref, gp_ref, gpt_ref, cos_ref, sin_ref,
         aq_ref, ak_ref, av_ref, bq_ref, bk_ref, bv_ref, z_ref, gc_ref, gr_ref, cbuf) = refs
    else:
        (xp_ref, x_ref, xn_ref, mod_ref, w_ref, wgt_ref, cw_ref, gp_ref, gpt_ref,
         aq_ref, ak_ref, av_ref, bq_ref, bk_ref, bv_ref, z_ref, gc_ref, gr_ref, cbuf) = refs
    i = pl.program_id(1)
    nt = pl.num_programs(1)
    mod = mod_ref[...]
    scale1, shift = 1.0 + mod[1:2], mod[0:1]
    h = (x_ref[...] * scale1 + shift).astype(BF16)
    proj = _dot(h, w_ref[...])

    if rope:
        cos2, sin2 = cos_ref[...], sin_ref[...]
        lane_lo = (lax.broadcasted_iota(jnp.int32, (tm, LANE), 1) % A_HD) < (A_HD // 2)
    for j in range(A_Q // LANE):
        a = proj[:, j * LANE:(j + 1) * LANE]
        if rope:
            a = _rope_half_tile(a, cos2, sin2, lane_lo)
        aq_ref[:, j * LANE:(j + 1) * LANE] = (a * (A_HD ** -0.5 * LOG2E)).astype(BF16)
    a = proj[:, OFF_AK:OFF_AK + LANE]
    if rope:
        a = _rope_half_tile(a, cos2, sin2, lane_lo)
    ak_ref[...] = a.astype(BF16)
    av_ref[...] = proj[:, OFF_AV:OFF_AV + LANE].astype(BF16)
    z_ref[...] = proj[:, OFF_BZ:OFF_BZ + B_VW]

    gp = gp_ref[...]
    lane = lax.broadcasted_iota(jnp.int32, (1, LANE), 1)
    gc_ref[...] = _gate_act(proj[:, OFF_GATE:OFF_GATE + LANE], lane < 2 * B_HEADS, gp[0:1], gp[1:2])
    gpt = gpt_ref[...]
    row = lax.broadcasted_iota(jnp.int32, (N_GATE, 1), 0)
    gr_ref[...] = _gate_act(_dot_nt(wgt_ref[...], h), row < 2 * B_HEADS, gpt[:, 0:1], gpt[:, 1:2])

    w_b = w_ref[:, OFF_BQKV:OFF_BQKV + B_QKV]
    hp = (xp_ref[...] * scale1 + shift).astype(BF16)
    hn = (xn_ref[...] * scale1 + shift).astype(BF16)
    cbuf[0:HALO, :] = _dot(hp, w_b) * jnp.where(i > 0, 1.0, 0.0)
    cbuf[HALO:HALO + tm, :] = proj[:, OFF_BQKV:OFF_BQKV + B_QKV]
    cbuf[HALO + tm:, :] = _dot(hn, w_b) * jnp.where(i < nt - 1, 1.0, 0.0)
    cw = cw_ref[...]
    pad = CONV_K // 2
    for part, dst in enumerate((bq_ref, bk_ref, bv_ref)):
        for hd in range(B_HEADS):
            c0 = part * B_QK + hd * B_DK
            acc = None
            for j in range(CONV_K):
                term = cbuf[HALO - pad + j:HALO - pad + j + tm, c0:c0 + B_DK] * cw[j:j + 1, c0:c0 + B_DK]
                acc = term if acc is None else acc + term
            acc = _silu(acc)
            if part < 2:
                acc = acc * lax.rsqrt(jnp.sum(acc * acc, axis=-1, keepdims=True) + NORM_EPS)
            if part == 0:
                acc = acc * (B_DK ** -0.5)
            dst[:, hd * B_DK:(hd + 1) * B_DK] = acc


def _ab_prep(x, mod, w_pad, w_gate_t, conv_w, gate_p, gate_pt, l, tables):
    bsz, t, d = x.shape
    tm = _token_tile(t)
    nh = tm // HALO
    nblk = t // HALO
    per_batch = mod.shape[0] > 1
    rope = tables is not None
    const = dict(pipeline_mode=pl.Buffered(1))
    in_specs = [pl.BlockSpec((None, HALO, d), lambda b, i: (b, jnp.maximum(i * nh - 1, 0), 0)),
                pl.BlockSpec((None, tm, d), lambda b, i: (b, i, 0)),
                pl.BlockSpec((None, HALO, d), lambda b, i: (b, jnp.minimum((i + 1) * nh, nblk - 1), 0)),
                pl.BlockSpec((None, 8, d), lambda b, i: (b if per_batch else 0, 0, 0)),
                pl.BlockSpec((None, d, AB_IN_PAD), lambda b, i: (l, 0, 0), **const),
                pl.BlockSpec((None, N_GATE, d), lambda b, i: (l, 0, 0), **const),
                pl.BlockSpec((None, 8, B_QKV), lambda b, i: (l, 0, 0)),
                pl.BlockSpec((None, 8, LANE), lambda b, i: (l, 0, 0)),
                pl.BlockSpec((None, N_GATE, LANE), lambda b, i: (l, 0, 0))]
    args = [x, x, x, mod, w_pad, w_gate_t, conv_w, gate_p, gate_pt]
    if rope:
        in_specs += [pl.BlockSpec((tm, LANE), lambda b, i: (i, 0))] * 2
        args += list(tables)
    tok = lambda w: pl.BlockSpec((None, tm, w), lambda b, i: (b, i, 0))
    shp = lambda w, dt: jax.ShapeDtypeStruct((bsz, t, w), dt)
    return pl.pallas_call(
        functools.partial(_ab_prep_body, rope=rope, tm=tm),
        grid=(bsz, t // tm),
        in_specs=in_specs,
        out_specs=[tok(A_Q), tok(A_KVW), tok(A_KVW), tok(B_QK), tok(B_QK), tok(B_VW), tok(B_VW), tok(LANE),
                   pl.BlockSpec((None, N_GATE, tm), lambda b, i: (b, 0, i))],
        out_shape=[shp(A_Q, BF16), shp(A_KVW, BF16), shp(A_KVW, BF16), shp(B_QK, F32), shp(B_QK, F32),
                   shp(B_VW, F32), shp(B_VW, F32), shp(LANE, F32),
                   jax.ShapeDtypeStruct((bsz, N_GATE, t), F32)],
        scratch_shapes=[pltpu.VMEM((tm + 2 * HALO, B_QKV), F32)],
        compiler_params=_cparams(("parallel", "parallel"), 52),
        name="ab_prep",
    )(*args)


A_GROUP = A_HEADS // A_KV


WIN_TILE = 1024
WIN_GROUP = 1
A_HEAD_PERM = tuple(h for t in range(A_GROUP) for h in (t, A_GROUP + t))


def _win_body(*refs, local, nqb):
    if local:
        q_ref, kp_ref, kc_ref, kn_ref, vp_ref, vc_ref, vn_ref, kx_ref, vx_ref, sink_ref, o_ref = refs
    else:
        q_ref, kx_ref, vx_ref, sink_ref, o_ref = refs
    lane = lax.broadcasted_iota(jnp.int32, (1, LANE), 1)
    kv_lanes = (lane < A_HD, lane >= A_HD)
    kx = kx_ref[...]
    vx = vx_ref[...]
    sink = sink_ref[...] * LOG2E
    if local:
        i = pl.program_id(1)
        nsteps = pl.num_programs(1)
        kcat = jnp.concatenate([kp_ref[...], kc_ref[...], kn_ref[...]], axis=0)
        vcat = jnp.concatenate([vp_ref[...], vc_ref[...], vn_ref[...]], axis=0)
        r = lax.broadcasted_iota(jnp.int32, (BLOCK, 3 * BLOCK), 0)
        c = lax.broadcasted_iota(jnp.int32, (BLOCK, 3 * BLOCK), 1)
        band = (c >= r) & (c <= r + 2 * WINDOW)
    zero = jnp.zeros((), BF16)
    sks = [jnp.concatenate([jnp.broadcast_to(sink[kv * A_GROUP + g:kv * A_GROUP + g + 1, 0:1], (BLOCK, 1))
                            for g in range(A_GROUP)], axis=0) for kv in range(A_KV)]
    def logits(qb0):
        units = []
        for qb in range(qb0, min(qb0 + WIN_GROUP, nqb)):
            r0 = qb * BLOCK
            tiles = [q_ref[r0:r0 + BLOCK, t * LANE:(t + 1) * LANE] for t in range(A_GROUP)]
            for kv in range(A_KV):
                qs = jnp.concatenate([jnp.where(kv_lanes[kv], tl, zero) for tl in tiles], axis=0)
                units.append(dict(qb=qb, kv=kv, r0=r0, qs=qs))
        for un in units:
            un["s_x"] = _dot_nt(un["qs"], kx)
            if local:
                un["s_l"] = _dot_nt(un["qs"], kcat[un["r0"]:un["r0"] + 3 * BLOCK])
        return units

    def finish(units):
        for un in units:
            m = jnp.maximum(jnp.max(un["s_x"], axis=-1, keepdims=True), sks[un["kv"]])
            if local:
                valid = band
                if un["qb"] == 0:
                    valid = valid & ((c >= BLOCK) | (i > 0))
                if un["qb"] == nqb - 1:
                    valid = valid & ((c < 2 * BLOCK) | (i < nsteps - 1))
                un["s_l"] = jnp.concatenate(
                    [jnp.where(valid, un["s_l"][g * BLOCK:(g + 1) * BLOCK], NEG) for g in range(A_GROUP)], axis=0)
                m = jnp.maximum(m, jnp.max(un["s_l"], axis=-1, keepdims=True))
            un["m"] = m
        for un in units:
            p_x = jnp.exp2(un["s_x"] - un["m"])
            un["den"] = jnp.sum(p_x, axis=-1, keepdims=True) + jnp.exp2(sks[un["kv"]] - un["m"])
            un["p_x"] = p_x.astype(BF16)
            if local:
                p_l = jnp.exp2(un["s_l"] - un["m"])
                un["den"] = un["den"] + jnp.sum(p_l, axis=-1, keepdims=True)
                un["p_l"] = p_l.astype(BF16)
        for un in units:
            o = _dot(un["p_x"], vx)
            if local:
                o = o + _dot(un["p_l"], vcat[un["r0"]:un["r0"] + 3 * BLOCK])
            un["o"] = o / un["den"]
        for u0 in range(0, len(units), A_KV):
            r0 = units[u0]["r0"]
            for t in range(A_GROUP):
                o_ref[r0:r0 + BLOCK, t * LANE:(t + 1) * LANE] = jnp.where(
                    kv_lanes[0], units[u0]["o"][t * BLOCK:(t + 1) * BLOCK],
                    units[u0 + 1]["o"][t * BLOCK:(t + 1) * BLOCK]).astype(BF16)

    starts = list(range(0, nqb, WIN_GROUP))
    pending = logits(starts[0])
    for nxt in starts[1:] + [None]:
        ahead = logits(nxt) if nxt is not None else None
        finish(pending)
        pending = ahead


def _win_attn(q, kx, vx, sink, l, k=None, v=None):
    bsz, t, _ = q.shape
    lc = kx.shape[1]
    local = k is not None
    tq = min(WIN_TILE, t)
    nqb = tq // BLOCK
    nb = t // BLOCK
    edge = lambda f: pl.BlockSpec((None, BLOCK, A_KVW), f)
    in_specs = [pl.BlockSpec((None, tq, A_Q), lambda b, i: (b, i, 0))]
    args = [q]
    if local:
        prv = lambda b, i: (b, jnp.maximum(i * nqb - 1, 0), 0)
        nxt = lambda b, i: (b, jnp.minimum((i + 1) * nqb, nb - 1), 0)
        cur = pl.BlockSpec((None, tq, A_KVW), lambda b, i: (b, i, 0))
        in_specs += [edge(prv), cur, edge(nxt), edge(prv), cur, edge(nxt)]
        args += [k, k, k, v, v, v]
    in_specs += [pl.BlockSpec((None, lc, A_KVW), lambda b, i: (b, 0, 0)),
                 pl.BlockSpec((None, lc, A_KVW), lambda b, i: (b, 0, 0)),
                 pl.BlockSpec((None, A_HEADS, LANE), lambda b, i: (l, 0, 0))]
    args += [kx, vx, sink]
    return pl.pallas_call(
        functools.partial(_win_body, local=local, nqb=nqb),
        grid=(bsz, t // tq),
        in_specs=in_specs,
        out_specs=pl.BlockSpec((None, tq, A_Q), lambda b, i: (b, i, 0)),
        out_shape=jax.ShapeDtypeStruct((bsz, t, A_Q), BF16),
        compiler_params=_cparams(("parallel", "parallel"), 40),
        name="win_attn",
    )(*args)


DELTA_BLOCK = 4 * CHUNK
DELTA_CHUNKS = DELTA_BLOCK // CHUNK
N_CHAIN = 2 * B_HEADS


def _split_bf16(a):
    hi = a.astype(BF16)
    return hi, (a - hi.astype(F32)).astype(BF16)


def _delta_intra(units, eye):
    for un in units:
        decay = jnp.exp(jnp.where(un["incl"], un["gcc"] - un["gcr"], NEG))
        kb = un["k"] * un["beta"]
        a = _dot_nt(jnp.concatenate([kb, un["q"]], axis=0).astype(BF16), un["k"].astype(BF16))
        un["p"] = jnp.where(un["strict"], a[:CHUNK] * decay, 0.0) * -1.0
        un["attn"] = (a[CHUNK:] * decay).astype(BF16)
        un["t"] = eye + un["p"]
        eg = jnp.exp(un["gcc"])
        un["rhs"] = jnp.concatenate([un["v"] * un["beta"], kb * eg], axis=1).astype(BF16)
        un["qd"] = un["q"] * eg
        un["kt_t"] = (un["k"] * jnp.exp(un["gtot"] - un["gcc"])).T.astype(BF16)
        un["dec"] = jnp.exp(un["gtot"])
    for un in units:
        pb = un["p"].astype(BF16)
        un["pb"] = _dot(pb, pb).astype(BF16)
    for _ in range(CHUNK.bit_length() - 3):
        for un in units:
            both = _dot(jnp.concatenate([un["pb"], un["t"].astype(BF16)], axis=0), un["pb"])
            un["t"] = un["t"] + both[CHUNK:]
            un["pb"] = both[:CHUNK].astype(BF16)
    for un in units:
        un["t"] = un["t"] + _dot(un["t"].astype(BF16), un["pb"])
    for un in units:
        uw = _dot(un["t"].astype(BF16), un["rhs"])
        un["u"] = uw[:, :B_DV]
        un["wq"] = jnp.concatenate([uw[:, B_DV:], un["qd"]], axis=0).astype(BF16)


def _delta_body(qf_ref, kf_ref, vf_ref, gcf_ref, grf_ref, qb_ref, kb_ref, vb_ref, gcb_ref, grb_ref, s0_ref,
                of_ref, ob_ref, s_ref):
    @pl.when(pl.program_id(1) == 0)
    def _():
        s_ref[...] = s0_ref[...]

    li = lax.broadcasted_iota(jnp.int32, (CHUNK, CHUNK), 0)
    lj = lax.broadcasted_iota(jnp.int32, (CHUNK, CHUNK), 1)
    eye = jnp.where(li == lj, 1.0, 0.0)
    tri_l = jnp.where(li >= lj, 1.0, 0.0).astype(BF16)
    tri_u = jnp.where(li <= lj, 1.0, 0.0).astype(BF16)

    dirs = ((qf_ref, kf_ref, vf_ref, gcf_ref, grf_ref, of_ref, tuple(range(DELTA_CHUNKS)), tri_l, tri_u, li >= lj, li > lj, CHUNK - 1),
            (qb_ref, kb_ref, vb_ref, gcb_ref, grb_ref, ob_ref, tuple(reversed(range(DELTA_CHUNKS))), tri_u, tri_l, li <= lj, li < lj, 0))
    units = {}
    for d, (q_ref, k_ref, v_ref, gc_ref, gr_ref, o_ref, order, tri_c, tri_r, incl, strict, last) in enumerate(dirs):
        for c in order:
            r0 = c * CHUNK
            gcol = gc_ref[r0:r0 + CHUNK, :]
            ghi, glo = _split_bf16(gcol)
            cum_c = _dot(tri_c, ghi) + _dot(tri_c, glo)
            grow = gr_ref[:, r0:r0 + CHUNK]
            rhi, rlo = _split_bf16(grow)
            cum_r = _dot(rhi, tri_r) + _dot(rlo, tri_r)
            for hd in range(B_HEADS):
                col = d * B_HEADS + hd
                gl = 2 * B_HEADS + col
                lo, hi = hd * B_DK, (hd + 1) * B_DK
                units[(d, c, hd)] = dict(
                    q=q_ref[r0:r0 + CHUNK, lo:hi], k=k_ref[r0:r0 + CHUNK, lo:hi], v=v_ref[r0:r0 + CHUNK, lo:hi],
                    beta=gcol[:, col:col + 1], gcc=cum_c[:, gl:gl + 1], gcr=cum_r[gl:gl + 1, :],
                    gtot=cum_c[last:last + 1, gl:gl + 1], incl=incl, strict=strict)
    _delta_intra(list(units.values()), eye)

    chains = [(d, hd) for d in range(2) for hd in range(B_HEADS)]
    state = [s_ref[d * B_HEADS + hd] for d, hd in chains]
    for step in range(DELTA_CHUNKS):
        cur = [units[(d, dirs[d][6][step], hd)] for d, hd in chains]
        ws = [_dot(un["wq"], s.astype(BF16)) for un, s in zip(cur, state)]
        v_new = [(un["u"] - w[:CHUNK]).astype(BF16) for un, w in zip(cur, ws)]
        outs = [w[CHUNK:] + _dot(un["attn"], vn) for un, w, vn in zip(cur, ws, v_new)]
        state = [s * un["dec"] + _dot(un["kt_t"], vn) for un, s, vn in zip(cur, state, v_new)]
        for (d, hd), o in zip(chains, outs):
            r0 = dirs[d][6][step] * CHUNK
            dirs[d][5][r0:r0 + CHUNK, hd * B_DV:(hd + 1) * B_DV] = o
    for (d, hd), s in zip(chains, state):
        s_ref[d * B_HEADS + hd] = s


def _delta(bq, bk, bv, gcol, grow, s0):
    bsz, t, _ = bq.shape
    ns = t // DELTA_BLOCK
    fwd = lambda w: pl.BlockSpec((None, DELTA_BLOCK, w), lambda b, s: (b, s, 0))
    bwd = lambda w: pl.BlockSpec((None, DELTA_BLOCK, w), lambda b, s: (b, ns - 1 - s, 0))
    st = pl.BlockSpec((None, N_CHAIN, B_DK, B_DV), lambda b, s: (b, 0, 0, 0))
    return pl.pallas_call(
        _delta_body,
        grid=(bsz, ns),
        in_specs=[fwd(B_QK), fwd(B_QK), fwd(B_VW), fwd(LANE),
                  pl.BlockSpec((None, N_GATE, DELTA_BLOCK), lambda b, s: (b, 0, s)),
                  bwd(B_QK), bwd(B_QK), bwd(B_VW), bwd(LANE),
                  pl.BlockSpec((None, N_GATE, DELTA_BLOCK), lambda b, s: (b, 0, ns - 1 - s)),
                  st],
        out_specs=[fwd(B_VW), bwd(B_VW), st],
        out_shape=[jax.ShapeDtypeStruct((bsz, t, B_VW), F32), jax.ShapeDtypeStruct((bsz, t, B_VW), F32),
                   jax.ShapeDtypeStruct((bsz, N_CHAIN, B_DK, B_DV), F32)],
        compiler_params=_cparams(("parallel", "arbitrary"), 32),
        name="delta",
    )(bq, bk, bv, gcol, grow, bq, bk, bv, gcol, grow, s0)


def _ab_out_body(x_ref, mod_ref, oa_ref, of_ref, ob_ref, z_ref, gn_ref, w_ref, out_ref):
    mod = mod_ref[...]
    gn = gn_ref[...]
    y = _dot(oa_ref[...], w_ref[0:A_Q, :])
    for hd in range(B_HEADS):
        lo, hi = hd * B_DV, (hd + 1) * B_DV
        o = of_ref[:, lo:hi] + ob_ref[:, lo:hi]
        o = o * lax.rsqrt(jnp.mean(o * o, axis=-1, keepdims=True) + NORM_EPS) * gn
        o = (o * _silu(z_ref[:, lo:hi])).astype(BF16)
        y = y + _dot(o, w_ref[A_Q + lo:A_Q + hi, :])
    out_ref[...] = _res_ln(x_ref[...], y, mod[2:3], mod[3:4], mod[4:5])


def _ab_out(x, mod, oa, of, ob, z, gnorm, w_out, l):
    bsz, t, d = x.shape
    tm = _token_tile(t)
    per_batch = mod.shape[0] > 1
    tok = lambda w: pl.BlockSpec((None, tm, w), lambda b, i: (b, i, 0))
    return pl.pallas_call(
        _ab_out_body,
        grid=(bsz, t // tm),
        in_specs=[tok(d),
                  pl.BlockSpec((None, 8, d), lambda b, i: (b if per_batch else 0, 0, 0)),
                  tok(A_Q), tok(B_VW), tok(B_VW), tok(B_VW),
                  pl.BlockSpec((None, 1, B_DV), lambda b, i: (l, 0, 0)),
                  pl.BlockSpec((None, A_Q + B_VW, d), lambda b, i: (l, 0, 0), pipeline_mode=pl.Buffered(1))],
        out_specs=tok(d),
        out_shape=jax.ShapeDtypeStruct(x.shape, F32),
        compiler_params=_cparams(("parallel", "parallel"), 40),
        name="ab_out",
    )(x, mod, oa, of, ob, z, gnorm, w_out)


def _rope_tables(rows, head_dim):
    n_freq = head_dim // 4
    inv = ROPE_THETA ** (-jnp.arange(n_freq, dtype=F32) / n_freq)
    r, col = jnp.meshgrid(jnp.arange(rows, dtype=F32), jnp.arange(GRID_W, dtype=F32), indexing='ij')
    r, col = r.reshape(-1), col.reshape(-1)
    ang = jnp.concatenate([r[:, None] * inv, col[:, None] * inv], axis=-1)
    cos, sin = jnp.cos(ang), jnp.sin(ang)
    cos2 = jnp.concatenate([cos, cos], axis=-1)
    sin2 = jnp.concatenate([-sin, sin], axis=-1)
    rep = LANE // head_dim
    return jnp.tile(cos2, (1, rep)), jnp.tile(sin2, (1, rep))


def _pad_rows(a, rows):
    return jnp.pad(a, ((0, 0), (0, rows - a.shape[1]), (0, 0)))


def kernel(x, c, ctx, c_ctx, ada_w, ada_b, ln_g, ln_b, ffn_w_gu, ffn_w_down, ab_w_in, ab_conv_w, ab_a_log,
           ab_dt_bias, ab_gnorm, ab_sink, ab_w_out, c_w_in, c_q_norm, c_k_norm, c_w_out):
    bsz, t, d = x.shape
    depth = ada_w.shape[0]
    n_even = ab_w_in.shape[0]
    rows = t // GRID_W
    tab_a = _rope_tables(rows, A_HD)
    tab_c = _rope_tables(rows, C_HD)

    wgu = ffn_w_gu.astype(BF16)
    wd = ffn_w_down.astype(BF16)
    head_cols = jnp.asarray([h * A_HD + j for h in A_HEAD_PERM for j in range(A_HD)], jnp.int32)
    ab_w_in_p = jnp.concatenate([jnp.take(ab_w_in[:, :, :A_Q], head_cols, axis=2), ab_w_in[:, :, A_Q:]], axis=2)
    ab_w_pad = jnp.pad(ab_w_in_p, ((0, 0), (0, 0), (0, AB_IN_PAD - AB_IN))).astype(BF16)
    ab_w_gate_t = jnp.swapaxes(ab_w_in[:, :, OFF_GATE:OFF_GATE + N_GATE], 1, 2).astype(BF16)
    ab_wo = jnp.concatenate([jnp.take(ab_w_out[:, :A_Q], head_cols, axis=1), ab_w_out[:, A_Q:]], axis=1).astype(BF16)
    c_wi = c_w_in.astype(BF16)
    c_wo = c_w_out.astype(BF16)
    conv_w = _pad_rows(ab_conv_w, 8)
    zeros8 = jnp.zeros((n_even, 2 * B_HEADS), F32)
    a_log16 = jnp.concatenate([zeros8, ab_a_log.reshape(n_even, 2 * B_HEADS)], axis=1)
    dtb16 = jnp.concatenate([zeros8, ab_dt_bias.reshape(n_even, 2 * B_HEADS)], axis=1)
    gate_p = _pad_rows(jnp.pad(jnp.stack([a_log16, dtb16], axis=1), ((0, 0), (0, 0), (0, LANE - N_GATE))), 8)
    gate_pt = jnp.pad(jnp.stack([a_log16, dtb16], axis=2), ((0, 0), (0, 0), (0, LANE - 2)))
    sink = jnp.broadcast_to(ab_sink[:, :, None], (n_even, A_HEADS, LANE))
    gnorm = ab_gnorm.reshape(n_even, 1, B_DV)
    qn = c_q_norm.reshape(-1, 1, C_HD)
    kn = c_k_norm.reshape(-1, 1, C_HD)

    nrow = ((bsz + 1 + 7) // 8) * 8
    cvec = jnp.pad(jnp.concatenate([c, c_ctx[None]], axis=0), ((0, nrow - bsz - 1), (0, 0)))
    m = _modulation(cvec, ada_w, ada_b).reshape(depth, nrow, N_MOD, d)

    def mod_rows(l, s):
        ln = jnp.stack([ln_g[l, s], ln_b[l, s]], axis=0)
        rows_l = jnp.concatenate([m[l, :bsz, 3 * s:3 * s + 3], jnp.broadcast_to(ln, (bsz, 2, d)),
                                  jnp.zeros((bsz, 3, d), F32)], axis=1)
        rows_c = jnp.concatenate([m[l, bsz:bsz + 1, 3 * s:3 * s + 3], ln[None],
                                  jnp.zeros((1, 3, d), F32)], axis=1)
        return rows_l, rows_c

    xl, xc = x, ctx
    for l in range(depth):
        ctx_out = l < depth - 1
        i = l // 2
        m0l, m0c = mod_rows(l, 0)
        m1l, m1c = mod_rows(l, 1)
        m2l, m2c = mod_rows(l, 2)
        xl = _ffn(xl, m0l, wgu, wd, l, 0)
        xc = _ffn(xc, m0c, wgu, wd, l, 0)
        if l % 2 == 0:
            pc = _ab_prep(xc, m1c, ab_w_pad, ab_w_gate_t, conv_w, gate_p, gate_pt, i, None)
            pl_ = _ab_prep(xl, m1l, ab_w_pad, ab_w_gate_t, conv_w, gate_p, gate_pt, i, tab_a)
            aqc, akc, avc, bqc, bkc, bvc, zc, gcc, grc = pc
            aql, akl, avl, bql, bkl, bvl, zl, gcl, grl = pl_
            ol_a = _win_attn(aql, akc, avc, sink, i, akl, avl)
            s0 = jnp.zeros((bsz, N_CHAIN, B_DK, B_DV), F32)
            oc_f, oc_b, s_ctx = _delta(bqc, bkc, bvc, gcc, grc, s0)
            ol_f, ol_b, _ = _delta(bql, bkl, bvl, gcl, grl, s_ctx)
            xl = _ab_out(xl, m1l, ol_a, ol_f, ol_b, zl, gnorm, ab_wo, i)
            if ctx_out:
                oc_a = _win_attn(aqc, akc, avc, sink, i)
                xc = _ab_out(xc, m1c, oc_a, oc_f, oc_b, zc, gnorm, ab_wo, i)
        else:
            qc, kc, vc = _c_prep(xc, m1c, c_wi, i, qn[i], kn[i], None)
            ql, kl, vl = _c_prep(xl, m1l, c_wi, i, qn[i], kn[i], tab_c)
            ol = _flash(ql, [kc, kl], [vc, vl])
            xl = _c_out(xl, m1l, ol, c_wo, i)
            if ctx_out:
                oc = _flash(qc, [kc], [vc])
                xc = _c_out(xc, m1c, oc, c_wo, i)
        xl = _ffn(xl, m2l, wgu, wd, l, 2 - 1)
        if ctx_out:
            xc = _ffn(xc, m2c, wgu, wd, l, 1)
    return xl
```

```python
import functools

import jax
import jax.numpy as jnp
from jax import lax
from jax.experimental import pallas as pl
from jax.experimental.pallas import tpu as pltpu

F32 = jnp.float32
BF16 = jnp.bfloat16

D_MODEL = 1024
DEPTH = 4
GRID_W = 64
D_FF = 2816
N_SUB = 3
N_MOD = 3 * N_SUB
FFN_HALF = 0.5
NORM_EPS = 1e-6
ROPE_THETA = 10000.0
DEEP_ALPHA = (2 * DEPTH) ** 0.25

A_HEADS, A_KV, A_HD = 8, 2, 64
WINDOW = 128
BLOCK = 128
B_HEADS, B_DK, B_DV = 4, 128, 128
CONV_K = 5
CHUNK = 64
C_HEADS, C_KV, C_HD = 8, 2, 128

A_Q = A_HEADS * A_HD
A_KVW = A_KV * A_HD
B_QK = B_HEADS * B_DK
B_VW = B_HEADS * B_DV
B_QKV = 2 * B_QK + B_VW
AB_IN = A_Q + 2 * A_KVW + B_QKV + B_VW + 4 * B_HEADS
N_GATE = 4 * B_HEADS
LANE = 128
AB_IN_PAD = AB_IN - N_GATE + LANE
OFF_AK = A_Q
OFF_AV = A_Q + A_KVW
OFF_BQKV = A_Q + 2 * A_KVW
OFF_BZ = OFF_BQKV + B_QKV
OFF_GATE = OFF_BZ + B_VW
C_QW = C_HEADS * C_HD
C_KW = C_KV * C_HD
C_IN = C_QW + 2 * C_KW
HALO = 8
NEG = -1e30
MIB = 1024 * 1024


def _cparams(sem, vmem_mib):
    return pltpu.CompilerParams(dimension_semantics=sem, vmem_limit_bytes=vmem_mib * MIB)


def _dot(a, b):
    return jnp.dot(a, b, preferred_element_type=F32)


def _dot_nt(a, b):
    return lax.dot_general(a, b, (((1,), (1,)), ((), ())), preferred_element_type=F32)


def _silu(x):
    return x * jax.nn.sigmoid(x)


def _res_ln(x, y, gate, g, b):
    z = DEEP_ALPHA * x + gate * y
    mu = jnp.mean(z, axis=-1, keepdims=True)
    zc = z - mu
    var = jnp.mean(zc * zc, axis=-1, keepdims=True)
    return zc * lax.rsqrt(var + NORM_EPS) * g + b


def _token_tile(t):
    return 512 if t % 512 == 0 else 256


def _mod_body(c_ref, w_ref, b_ref, o_ref):
    c = c_ref[...]
    o_ref[...] = jnp.dot(_silu(c), w_ref[...], preferred_element_type=F32,
                         precision=lax.Precision.HIGHEST) + b_ref[...]


def _modulation(cvec, ada_w, ada_b):
    depth, d, n = ada_w.shape
    rows = cvec.shape[0]
    tn = 2304
    return pl.pallas_call(
        _mod_body,
        grid=(depth, n // tn),
        in_specs=[pl.BlockSpec((rows, d), lambda l, j: (0, 0)),
                  pl.BlockSpec((None, d, tn), lambda l, j: (l, 0, j)),
                  pl.BlockSpec((None, 1, tn), lambda l, j: (l, 0, j))],
        out_specs=pl.BlockSpec((None, rows, tn), lambda l, j: (l, 0, j)),
        out_shape=jax.ShapeDtypeStruct((depth, rows, n), F32),
        compiler_params=_cparams(("parallel", "parallel"), 40),
        name="mod",
    )(cvec, ada_w, ada_b.reshape(depth, 1, n))


FFN_CHUNK = 256


def _ffn_body(x_ref, mod_ref, wgu_ref, wd_ref, o_ref):
    x = x_ref[...]
    mod = mod_ref[...]
    h = (x * (1.0 + mod[1:2]) + mod[0:1]).astype(BF16)
    fc = FFN_CHUNK if x.shape[0] >= 512 else D_FF // 2
    y = None
    for j in range(D_FF // fc):
        gt = _dot(h, wgu_ref[:, j * fc:(j + 1) * fc])
        up = _dot(h, wgu_ref[:, D_FF + j * fc:D_FF + (j + 1) * fc])
        act = (_silu(gt) * up).astype(BF16)
        part = _dot(act, wd_ref[j * fc:(j + 1) * fc, :])
        y = part if y is None else y + part
    o_ref[...] = _res_ln(x, y, FFN_HALF * mod[2:3], mod[3:4], mod[4:5])


def _ffn(x, mod, wgu, wd, l, s):
    bsz, t, d = x.shape
    tm = _token_tile(t)
    per_batch = mod.shape[0] > 1
    return pl.pallas_call(
        _ffn_body,
        grid=(bsz, t // tm),
        in_specs=[pl.BlockSpec((None, tm, d), lambda b, i: (b, i, 0)),
                  pl.BlockSpec((None, 8, d), lambda b, i: (b if per_batch else 0, 0, 0)),
                  pl.BlockSpec((None, None, d, 2 * D_FF), lambda b, i: (l, s, 0, 0),
                               pipeline_mode=pl.Buffered(1)),
                  pl.BlockSpec((None, None, D_FF, d), lambda b, i: (l, s, 0, 0),
                               pipeline_mode=pl.Buffered(1))],
        out_specs=pl.BlockSpec((None, tm, d), lambda b, i: (b, i, 0)),
        out_shape=jax.ShapeDtypeStruct(x.shape, F32),
        compiler_params=_cparams(("parallel", "parallel"), 52),
        name="ffn",
    )(x, mod, wgu, wd)


def _rope_full(x, cos2, sin2):
    return x * cos2 + pltpu.roll(x, C_HD // 2, 1) * sin2


def _c_prep_body(*refs, rope):
    if rope:
        x_ref, mod_ref, w_ref, qn_ref, kn_ref, cos_ref, sin_ref, q_ref, k_ref, v_ref = refs
        cos2, sin2 = cos_ref[...], sin_ref[...]
    else:
        x_ref, mod_ref, w_ref, qn_ref, kn_ref, q_ref, k_ref, v_ref = refs
    x = x_ref[...]
    mod = mod_ref[...]
    h = (x * (1.0 + mod[1:2]) + mod[0:1]).astype(BF16)
    qkv = _dot(h, w_ref[...])
    qn = qn_ref[...] * (C_HD ** -0.5 * LOG2E)
    kn = kn_ref[...]

    heads = [(qkv[:, j * C_HD:(j + 1) * C_HD], qn) for j in range(C_HEADS)]
    heads += [(qkv[:, C_QW + j * C_HD:C_QW + (j + 1) * C_HD], kn) for j in range(C_KV)]
    ms = [jnp.mean(a * a, axis=-1, keepdims=True) for a, _ in heads]
    outs = [a * lax.rsqrt(m + NORM_EPS) * gain for (a, gain), m in zip(heads, ms)]
    if rope:
        outs = [_rope_full(a, cos2, sin2) for a in outs]
    for j in range(C_HEADS):
        q_ref[:, j * C_HD:(j + 1) * C_HD] = outs[j].astype(BF16)
    for j in range(C_KV):
        k_ref[:, j * C_HD:(j + 1) * C_HD] = outs[C_HEADS + j].astype(BF16)
    v_ref[...] = qkv[:, C_QW + C_KW:].astype(BF16)


def _c_prep(x, mod, w_in, l, qn, kn, tables):
    bsz, t, d = x.shape
    tm = _token_tile(t)
    per_batch = mod.shape[0] > 1
    rope = tables is not None
    in_specs = [pl.BlockSpec((None, tm, d), lambda b, i: (b, i, 0)),
                pl.BlockSpec((None, 8, d), lambda b, i: (b if per_batch else 0, 0, 0)),
                pl.BlockSpec((None, d, C_IN), lambda b, i: (l, 0, 0), pipeline_mode=pl.Buffered(1)),
                pl.BlockSpec((1, C_HD), lambda b, i: (0, 0)),
                pl.BlockSpec((1, C_HD), lambda b, i: (0, 0))]
    args = [x, mod, w_in, qn, kn]
    if rope:
        in_specs += [pl.BlockSpec((tm, C_HD), lambda b, i: (i, 0))] * 2
        args += list(tables)
    return pl.pallas_call(
        functools.partial(_c_prep_body, rope=rope),
        grid=(bsz, t // tm),
        in_specs=in_specs,
        out_specs=[pl.BlockSpec((None, tm, C_QW), lambda b, i: (b, i, 0)),
                   pl.BlockSpec((None, tm, C_KW), lambda b, i: (b, i, 0)),
                   pl.BlockSpec((None, tm, C_KW), lambda b, i: (b, i, 0))],
        out_shape=[jax.ShapeDtypeStruct((bsz, t, C_QW), BF16),
                   jax.ShapeDtypeStruct((bsz, t, C_KW), BF16),
                   jax.ShapeDtypeStruct((bsz, t, C_KW), BF16)],
        compiler_params=_cparams(("parallel", "parallel"), 40),
        name="c_prep",
    )(*args)


C_GROUP = C_HEADS // C_KV
TQ_C = 256
KC_C = 256
RB_C = 64
FLASH_UNROLL = 4
LOG2E = 1.4426950408889634


def _flash_body(q_ref, k_ref, v_ref, o_ref, q_scr, s_scr, p0_scr, p1_scr, m_scr, mb_scr, l_scr, acc_scr, *,
                nch, tail_keys):
    i = pl.program_id(2)
    slot_a = i % 2
    slot_b = 1 - slot_a
    rows = q_scr.shape[0]
    tq = q_ref.shape[0]

    @pl.when(i == 0)
    def _():
        s_scr[1] = jnp.zeros(s_scr.shape[1:], F32)
        m_scr[1] = jnp.zeros(m_scr.shape[1:], F32)

    for g in range(C_GROUP):
        q_scr[g * tq:(g + 1) * tq, :] = q_ref[:, g * C_HD:(g + 1) * C_HD]
    m_cur = jnp.max(m_scr[slot_b], axis=-1, keepdims=True)
    mb_scr[...] = jnp.broadcast_to(m_cur, (rows, LANE))
    m_scr[slot_a] = jnp.full((rows, LANE), NEG, F32)
    l_scr[...] = jnp.zeros((rows, LANE), F32)
    acc_scr[...] = jnp.zeros((rows, C_HD), F32)

    def probs(j, p_scr):
        for r0 in range(0, rows, RB_C):
            mb = mb_scr[r0:r0 + RB_C, :]
            lt = l_scr[r0:r0 + RB_C, :]
            for c0 in range(0, KC_C, LANE):
                p = jnp.exp2(s_scr[slot_b, j, r0:r0 + RB_C, c0:c0 + LANE] - mb)
                lt = lt + p
                p_scr[r0:r0 + RB_C, c0:c0 + LANE] = p.astype(BF16)
            l_scr[r0:r0 + RB_C, :] = lt

    def pv(j, p_scr):
        k0 = pl.multiple_of(j * KC_C, KC_C)
        acc_scr[...] += _dot(p_scr[...], v_ref[pl.ds(k0, KC_C), :])

    def logits(j, n_keys=KC_C):
        k0 = pl.multiple_of(j * KC_C, KC_C)
        s = _dot_nt(q_scr[...], k_ref[pl.ds(k0, KC_C), :])
        if n_keys < KC_C:
            s = jnp.where(lax.broadcasted_iota(jnp.int32, (1, KC_C), 1) < n_keys, s, NEG)
        s_scr[slot_a, j] = s
        mt = s[:, 0:LANE]
        for c0 in range(LANE, KC_C, LANE):
            mt = jnp.maximum(mt, s[:, c0:c0 + LANE])
        m_scr[slot_a] = jnp.maximum(m_scr[slot_a], mt)

    probs(0, p0_scr)

    def pair(t, carry):
        j = 2 * t
        logits(j)
        pv(j, p0_scr)
        probs(j + 1, p1_scr)
        logits(j + 1)
        pv(j + 1, p1_scr)
        probs(j + 2, p0_scr)
        return carry

    lax.fori_loop(0, (nch - 1) // 2, pair, 0, unroll=FLASH_UNROLL)
    pv(nch - 1, p0_scr)
    logits(nch - 1, tail_keys)

    o = acc_scr[...] / jnp.sum(l_scr[...], axis=-1, keepdims=True)
    for g in range(C_GROUP):
        o_ref[:, g * C_HD:(g + 1) * C_HD] = o[g * tq:(g + 1) * tq].astype(BF16)


def _flash(q, ks, vs):
    bsz, t, _ = q.shape
    n_keys = sum(a.shape[1] for a in ks)
    nch = -(-n_keys // KC_C)
    nch += 1 - nch % 2
    n = nch * KC_C
    tail_keys = n_keys - (nch - 1) * KC_C
    assert 0 < tail_keys <= KC_C
    pad = [jnp.zeros((bsz, n - n_keys, C_KW), BF16)] if n > n_keys else []
    k = jnp.concatenate(list(ks) + pad, axis=1)
    v = jnp.concatenate(list(vs) + pad, axis=1)
    tq = TQ_C
    nq = t // tq
    rows = C_GROUP * tq
    gw = C_GROUP * C_HD
    return pl.pallas_call(
        functools.partial(_flash_body, nch=nch, tail_keys=tail_keys),
        grid=(bsz, C_KV, nq + 1),
        in_specs=[pl.BlockSpec((None, tq, gw), lambda b, h, i: (b, jnp.minimum(i, nq - 1), h)),
                  pl.BlockSpec((None, n, C_HD), lambda b, h, i: (b, 0, h)),
                  pl.BlockSpec((None, n, C_HD), lambda b, h, i: (b, 0, h))],
        out_specs=pl.BlockSpec((None, tq, gw), lambda b, h, i: (b, jnp.maximum(i - 1, 0), h)),
        out_shape=jax.ShapeDtypeStruct((bsz, t, C_QW), BF16),
        scratch_shapes=[pltpu.VMEM((rows, C_HD), BF16),
                        pltpu.VMEM((2, nch, rows, KC_C), F32),
                        pltpu.VMEM((rows, KC_C), BF16),
                        pltpu.VMEM((rows, KC_C), BF16),
                        pltpu.VMEM((2, rows, LANE), F32),
                        pltpu.VMEM((rows, LANE), F32),
                        pltpu.VMEM((rows, LANE), F32),
                        pltpu.VMEM((rows, C_HD), F32)],
        compiler_params=_cparams(("parallel", "parallel", "arbitrary"), 56),
        name="flash",
    )(q, k, v)


def _c_out_body(x_ref, mod_ref, o_ref, w_ref, out_ref):
    mod = mod_ref[...]
    y = _dot(o_ref[...], w_ref[...])
    out_ref[...] = _res_ln(x_ref[...], y, mod[2:3], mod[3:4], mod[4:5])


def _c_out(x, mod, o, w_out, l):
    bsz, t, d = x.shape
    tm = _token_tile(t)
    per_batch = mod.shape[0] > 1
    return pl.pallas_call(
        _c_out_body,
        grid=(bsz, t // tm),
        in_specs=[pl.BlockSpec((None, tm, d), lambda b, i: (b, i, 0)),
                  pl.BlockSpec((None, 8, d), lambda b, i: (b if per_batch else 0, 0, 0)),
                  pl.BlockSpec((None, tm, C_QW), lambda b, i: (b, i, 0)),
                  pl.BlockSpec((None, C_QW, d), lambda b, i: (l, 0, 0), pipeline_mode=pl.Buffered(1))],
        out_specs=pl.BlockSpec((None, tm, d), lambda b, i: (b, i, 0)),
        out_shape=jax.ShapeDtypeStruct(x.shape, F32),
        compiler_params=_cparams(("parallel", "parallel"), 32),
        name="c_out",
    )(x, mod, o, w_out)


def _rope_half_tile(x, cos2, sin2, lane_lo):
    half = A_HD // 2
    partner = jnp.where(lane_lo, pltpu.roll(x, LANE - half, 1), pltpu.roll(x, half, 1))
    return x * cos2 + partner * sin2


def _gate_act(raw, is_beta, a_log, dt_bias):
    sp_in = raw + dt_bias
    softplus = jnp.maximum(sp_in, 0.0) + jnp.log(1.0 + jnp.exp(-jnp.abs(sp_in)))
    return jnp.where(is_beta, jax.nn.sigmoid(raw), -jnp.exp(a_log) * softplus)


def _ab_prep_body(*refs, rope, tm):
    if rope:
        (xp_ref, x_ref, xn_ref, mod_ref, w_ref, cw_ref, gp_ref, cos_ref, sin_ref,
         aq_ref, ak_ref, av_ref, bq_ref, bk_ref, bv_ref, z_ref, gc_ref, gr_ref, cbuf) = refs
    else:
        (xp_ref, x_ref, xn_ref, mod_ref, w_ref, cw_ref, gp_ref,
         aq_ref, ak_ref, av_ref, bq_ref, bk_ref, bv_ref, z_ref, gc_ref, gr_ref, cbuf) = refs
    i = pl.program_id(1)
    nt = pl.num_programs(1)
    mod = mod_ref[...]
    scale1, shift = 1.0 + mod[1:2], mod[0:1]
    h = (x_ref[...] * scale1 + shift).astype(BF16)
    halo = jnp.concatenate([xp_ref[...], xn_ref[...]], axis=0)
    h_all = jnp.concatenate([h, (halo * scale1 + shift).astype(BF16)], axis=0)
    proj_all = _dot(h_all, w_ref[...])
    proj = proj_all[:tm]

    if rope:
        cos2, sin2 = cos_ref[...], sin_ref[...]
        lane_lo = (lax.broadcasted_iota(jnp.int32, (tm, LANE), 1) % A_HD) < (A_HD // 2)
    for j in range(A_Q // LANE):
        a = proj[:, j * LANE:(j + 1) * LANE]
        if rope:
            a = _rope_half_tile(a, cos2, sin2, lane_lo)
        aq_ref[:, j * LANE:(j + 1) * LANE] = (a * (A_HD ** -0.5 * LOG2E)).astype(BF16)
    a = proj[:, OFF_AK:OFF_AK + LANE]
    if rope:
        a = _rope_half_tile(a, cos2, sin2, lane_lo)
    ak_ref[...] = a.astype(BF16)
    av_ref[...] = proj[:, OFF_AV:OFF_AV + LANE].astype(BF16)
    z_ref[...] = proj[:, OFF_BZ:OFF_BZ + B_VW]

    gp = gp_ref[...]
    lane = lax.broadcasted_iota(jnp.int32, (1, LANE), 1)
    gates = _gate_act(proj[:, OFF_GATE:OFF_GATE + LANE], lane < 2 * B_HEADS, gp[0:1], gp[1:2])
    gc_ref[...] = gates
    gr_ref[...] = gates.T[:N_GATE, :]

    cbuf[0:HALO, :] = proj_all[tm:tm + HALO, OFF_BQKV:OFF_BQKV + B_QKV] * jnp.where(i > 0, 1.0, 0.0)
    cbuf[HALO:HALO + tm, :] = proj[:, OFF_BQKV:OFF_BQKV + B_QKV]
    cbuf[HALO + tm:, :] = proj_all[tm + HALO:, OFF_BQKV:OFF_BQKV + B_QKV] * jnp.where(i < nt - 1, 1.0, 0.0)
    cw = cw_ref[...]
    pad = CONV_K // 2
    for part, dst in enumerate((bq_ref, bk_ref, bv_ref)):
        for hd in range(B_HEADS):
            c0 = part * B_QK + hd * B_DK
            acc = None
            for j in range(CONV_K):
                term = cbuf[HALO - pad + j:HALO - pad + j + tm, c0:c0 + B_DK] * cw[j:j + 1, c0:c0 + B_DK]
                acc = term if acc is None else acc + term
            acc = _silu(acc)
            if part < 2:
                acc = acc * lax.rsqrt(jnp.sum(acc * acc, axis=-1, keepdims=True) + NORM_EPS)
            if part == 0:
                acc = acc * (B_DK ** -0.5)
            dst[:, hd * B_DK:(hd + 1) * B_DK] = acc


def _ab_prep(x, mod, w_pad, conv_w, gate_p, l, tables):
    bsz, t, d = x.shape
    tm = _token_tile(t)
    nh = tm // HALO
    nblk = t // HALO
    per_batch = mod.shape[0] > 1
    rope = tables is not None
    const = dict(pipeline_mode=pl.Buffered(1))
    in_specs = [pl.BlockSpec((None, HALO, d), lambda b, i: (b, jnp.maximum(i * nh - 1, 0), 0)),
                pl.BlockSpec((None, tm, d), lambda b, i: (b, i, 0)),
                pl.BlockSpec((None, HALO, d), lambda b, i: (b, jnp.minimum((i + 1) * nh, nblk - 1), 0)),
                pl.BlockSpec((None, 8, d), lambda b, i: (b if per_batch else 0, 0, 0)),
                pl.BlockSpec((None, d, AB_IN_PAD), lambda b, i: (l, 0, 0), **const),
                pl.BlockSpec((None, 8, B_QKV), lambda b, i: (l, 0, 0)),
                pl.BlockSpec((None, 8, LANE), lambda b, i: (l, 0, 0))]
    args = [x, x, x, mod, w_pad, conv_w, gate_p]
    if rope:
        in_specs += [pl.BlockSpec((tm, LANE), lambda b, i: (i, 0))] * 2
        args += list(tables)
    tok = lambda w: pl.BlockSpec((None, tm, w), lambda b, i: (b, i, 0))
    shp = lambda w, dt: jax.ShapeDtypeStruct((bsz, t, w), dt)
    return pl.pallas_call(
        functools.partial(_ab_prep_body, rope=rope, tm=tm),
        grid=(bsz, t // tm),
        in_specs=in_specs,
        out_specs=[tok(A_Q), tok(A_KVW), tok(A_KVW), tok(B_QK), tok(B_QK), tok(B_VW), tok(B_VW), tok(LANE),
                   pl.BlockSpec((None, N_GATE, tm), lambda b, i: (b, 0, i))],
        out_shape=[shp(A_Q, BF16), shp(A_KVW, BF16), shp(A_KVW, BF16), shp(B_QK, F32), shp(B_QK, F32),
                   shp(B_VW, F32), shp(B_VW, F32), shp(LANE, F32),
                   jax.ShapeDtypeStruct((bsz, N_GATE, t), F32)],
        scratch_shapes=[pltpu.VMEM((tm + 2 * HALO, B_QKV), F32)],
        compiler_params=_cparams(("parallel", "parallel"), 52),
        name="ab_prep",
    )(*args)


A_GROUP = A_HEADS // A_KV


WIN_TILE = 1024
WIN_GROUP = 1
A_HEAD_PERM = tuple(h for t in range(A_GROUP) for h in (t, A_GROUP + t))


def _win_body(*refs, local, nqb):
    if local:
        q_ref, kp_ref, kc_ref, kn_ref, vp_ref, vc_ref, vn_ref, kx_ref, vx_ref, sink_ref, o_ref = refs
    else:
        q_ref, kx_ref, vx_ref, sink_ref, o_ref = refs
    lane = lax.broadcasted_iota(jnp.int32, (1, LANE), 1)
    kv_lanes = (lane < A_HD, lane >= A_HD)
    kx = kx_ref[...]
    vx = vx_ref[...]
    sink = sink_ref[...] * LOG2E
    if local:
        i = pl.program_id(1)
        nsteps = pl.num_programs(1)
        kcat = jnp.concatenate([kp_ref[...], kc_ref[...], kn_ref[...]], axis=0)
        vcat = jnp.concatenate([vp_ref[...], vc_ref[...], vn_ref[...]], axis=0)
        r = lax.broadcasted_iota(jnp.int32, (BLOCK, 3 * BLOCK), 0)
        c = lax.broadcasted_iota(jnp.int32, (BLOCK, 3 * BLOCK), 1)
        band = (c >= r) & (c <= r + 2 * WINDOW)
    zero = jnp.zeros((), BF16)
    sks = [jnp.concatenate([jnp.broadcast_to(sink[kv * A_GROUP + g:kv * A_GROUP + g + 1, 0:1], (BLOCK, 1))
                            for g in range(A_GROUP)], axis=0) for kv in range(A_KV)]
    def logits(qb0):
        units = []
        for qb in range(qb0, min(qb0 + WIN_GROUP, nqb)):
            r0 = qb * BLOCK
            tiles = [q_ref[r0:r0 + BLOCK, t * LANE:(t + 1) * LANE] for t in range(A_GROUP)]
            for kv in range(A_KV):
                qs = jnp.concatenate([jnp.where(kv_lanes[kv], tl, zero) for tl in tiles], axis=0)
                units.append(dict(qb=qb, kv=kv, r0=r0, qs=qs))
        for un in units:
            un["s_x"] = _dot_nt(un["qs"], kx)
            if local:
                un["s_l"] = _dot_nt(un["qs"], kcat[un["r0"]:un["r0"] + 3 * BLOCK])
        return units

    def finish(units):
        for un in units:
            m = jnp.maximum(jnp.max(un["s_x"], axis=-1, keepdims=True), sks[un["kv"]])
            if local:
                valid = band
                if un["qb"] == 0:
                    valid = valid & ((c >= BLOCK) | (i > 0))
                if un["qb"] == nqb - 1:
                    valid = valid & ((c < 2 * BLOCK) | (i < nsteps - 1))
                un["s_l"] = jnp.concatenate(
                    [jnp.where(valid, un["s_l"][g * BLOCK:(g + 1) * BLOCK], NEG) for g in range(A_GROUP)], axis=0)
                m = jnp.maximum(m, jnp.max(un["s_l"], axis=-1, keepdims=True))
            un["m"] = m
        for un in units:
            p_x = jnp.exp2(un["s_x"] - un["m"])
            un["den"] = jnp.sum(p_x, axis=-1, keepdims=True) + jnp.exp2(sks[un["kv"]] - un["m"])
            un["p_x"] = p_x.astype(BF16)
            if local:
                p_l = jnp.exp2(un["s_l"] - un["m"])
                un["den"] = un["den"] + jnp.sum(p_l, axis=-1, keepdims=True)
                un["p_l"] = p_l.astype(BF16)
        for un in units:
            o = _dot(un["p_x"], vx)
            if local:
                o = o + _dot(un["p_l"], vcat[un["r0"]:un["r0"] + 3 * BLOCK])
            un["o"] = o / un["den"]
        for u0 in range(0, len(units), A_KV):
            r0 = units[u0]["r0"]
            for t in range(A_GROUP):
                o_ref[r0:r0 + BLOCK, t * LANE:(t + 1) * LANE] = jnp.where(
                    kv_lanes[0], units[u0]["o"][t * BLOCK:(t + 1) * BLOCK],
                    units[u0 + 1]["o"][t * BLOCK:(t + 1) * BLOCK]).astype(BF16)

    starts = list(range(0, nqb, WIN_GROUP))
    pending = logits(starts[0])
    for nxt in starts[1:] + [None]:
        ahead = logits(nxt) if nxt is not None else None
        finish(pending)
        pending = ahead


def _win_attn(q, kx, vx, sink, l, k=None, v=None):
    bsz, t, _ = q.shape
    lc = kx.shape[1]
    local = k is not None
    tq = min(WIN_TILE, t)
    nqb = tq // BLOCK
    nb = t // BLOCK
    edge = lambda f: pl.BlockSpec((None, BLOCK, A_KVW), f)
    in_specs = [pl.BlockSpec((None, tq, A_Q), lambda b, i: (b, i, 0))]
    args = [q]
    if local:
        prv = lambda b, i: (b, jnp.maximum(i * nqb - 1, 0), 0)
        nxt = lambda b, i: (b, jnp.minimum((i + 1) * nqb, nb - 1), 0)
        cur = pl.BlockSpec((None, tq, A_KVW), lambda b, i: (b, i, 0))
        in_specs += [edge(prv), cur, edge(nxt), edge(prv), cur, edge(nxt)]
        args += [k, k, k, v, v, v]
    in_specs += [pl.BlockSpec((None, lc, A_KVW), lambda b, i: (b, 0, 0)),
                 pl.BlockSpec((None, lc, A_KVW), lambda b, i: (b, 0, 0)),
                 pl.BlockSpec((None, A_HEADS, LANE), lambda b, i: (l, 0, 0))]
    args += [kx, vx, sink]
    return pl.pallas_call(
        functools.partial(_win_body, local=local, nqb=nqb),
        grid=(bsz, t // tq),
        in_specs=in_specs,
        out_specs=pl.BlockSpec((None, tq, A_Q), lambda b, i: (b, i, 0)),
        out_shape=jax.ShapeDtypeStruct((bsz, t, A_Q), BF16),
        compiler_params=_cparams(("parallel", "parallel"), 40),
        name="win_attn",
    )(*args)


DELTA_BLOCK = 4 * CHUNK
DELTA_CHUNKS = DELTA_BLOCK // CHUNK
N_CHAIN = 2 * B_HEADS


def _split_bf16(a):
    hi = a.astype(BF16)
    return hi, (a - hi.astype(F32)).astype(BF16)


def _delta_intra(units, eye):
    for un in units:
        decay = jnp.exp(jnp.where(un["incl"], un["gcc"] - un["gcr"], NEG))
        kb = un["k"] * un["beta"]
        a = _dot_nt(jnp.concatenate([kb, un["q"]], axis=0).astype(BF16), un["k"].astype(BF16))
        un["p"] = jnp.where(un["strict"], a[:CHUNK] * decay, 0.0) * -1.0
        un["attn"] = (a[CHUNK:] * decay).astype(BF16)
        un["t"] = eye + un["p"]
        eg = jnp.exp(un["gcc"])
        un["rhs"] = jnp.concatenate([un["v"] * un["beta"], kb * eg], axis=1).astype(BF16)
        un["qd"] = un["q"] * eg
        un["kt_t"] = (un["k"] * jnp.exp(un["gtot"] - un["gcc"])).T.astype(BF16)
        un["dec"] = jnp.exp(un["gtot"])
    for un in units:
        pb = un["p"].astype(BF16)
        un["pb"] = _dot(pb, pb).astype(BF16)
    for _ in range(CHUNK.bit_length() - 3):
        for un in units:
            both = _dot(jnp.concatenate([un["pb"], un["t"].astype(BF16)], axis=0), un["pb"])
            un["t"] = un["t"] + both[CHUNK:]
            un["pb"] = both[:CHUNK].astype(BF16)
    for un in units:
        un["t"] = un["t"] + _dot(un["t"].astype(BF16), un["pb"])
    for un in units:
        uw = _dot(un["t"].astype(BF16), un["rhs"])
        un["u"] = uw[:, :B_DV]
        un["wq"] = jnp.concatenate([uw[:, B_DV:], un["qd"]], axis=0).astype(BF16)


def _delta_body(qf_ref, kf_ref, vf_ref, gcf_ref, grf_ref, qb_ref, kb_ref, vb_ref, gcb_ref, grb_ref, s0_ref,
                of_ref, ob_ref, s_ref):
    @pl.when(pl.program_id(1) == 0)
    def _():
        s_ref[...] = s0_ref[...]

    li = lax.broadcasted_iota(jnp.int32, (CHUNK, CHUNK), 0)
    lj = lax.broadcasted_iota(jnp.int32, (CHUNK, CHUNK), 1)
    eye = jnp.where(li == lj, 1.0, 0.0)
    tri_l = jnp.where(li >= lj, 1.0, 0.0).astype(BF16)
    tri_u = jnp.where(li <= lj, 1.0, 0.0).astype(BF16)

    dirs = ((qf_ref, kf_ref, vf_ref, gcf_ref, grf_ref, of_ref, tuple(range(DELTA_CHUNKS)), tri_l, tri_u, li >= lj, li > lj, CHUNK - 1),
            (qb_ref, kb_ref, vb_ref, gcb_ref, grb_ref, ob_ref, tuple(reversed(range(DELTA_CHUNKS))), tri_u, tri_l, li <= lj, li < lj, 0))
    units = {}
    for d, (q_ref, k_ref, v_ref, gc_ref, gr_ref, o_ref, order, tri_c, tri_r, incl, strict, last) in enumerate(dirs):
        for c in order:
            r0 = c * CHUNK
            gcol = gc_ref[r0:r0 + CHUNK, :]
            ghi, glo = _split_bf16(gcol)
            cum_c = _dot(tri_c, ghi) + _dot(tri_c, glo)
            grow = gr_ref[:, r0:r0 + CHUNK]
            rhi, rlo = _split_bf16(grow)
            cum_r = _dot(rhi, tri_r) + _dot(rlo, tri_r)
            for hd in range(B_HEADS):
                col = d * B_HEADS + hd
                gl = 2 * B_HEADS + col
                lo, hi = hd * B_DK, (hd + 1) * B_DK
                units[(d, c, hd)] = dict(
                    q=q_ref[r0:r0 + CHUNK, lo:hi], k=k_ref[r0:r0 + CHUNK, lo:hi], v=v_ref[r0:r0 + CHUNK, lo:hi],
                    beta=gcol[:, col:col + 1], gcc=cum_c[:, gl:gl + 1], gcr=cum_r[gl:gl + 1, :],
                    gtot=cum_c[last:last + 1, gl:gl + 1], incl=incl, strict=strict)
    _delta_intra(list(units.values()), eye)

    chains = [(d, hd) for d in range(2) for hd in range(B_HEADS)]
    state = [s_ref[d * B_HEADS + hd] for d, hd in chains]
    for step in range(DELTA_CHUNKS):
        cur = [units[(d, dirs[d][6][step], hd)] for d, hd in chains]
        ws = [_dot(un["wq"], s.astype(BF16)) for un, s in zip(cur, state)]
        v_new = [(un["u"] - w[:CHUNK]).astype(BF16) for un, w in zip(cur, ws)]
        outs = [w[CHUNK:] + _dot(un["attn"], vn) for un, w, vn in zip(cur, ws, v_new)]
        state = [s * un["dec"] + _dot(un["kt_t"], vn) for un, s, vn in zip(cur, state, v_new)]
        for (d, hd), o in zip(chains, outs):
            r0 = dirs[d][6][step] * CHUNK
            dirs[d][5][r0:r0 + CHUNK, hd * B_DV:(hd + 1) * B_DV] = o
    for (d, hd), s in zip(chains, state):
        s_ref[d * B_HEADS + hd] = s


def _delta(bq, bk, bv, gcol, grow, s0):
    bsz, t, _ = bq.shape
    ns = t // DELTA_BLOCK
    fwd = lambda w: pl.BlockSpec((None, DELTA_BLOCK, w), lambda b, s: (b, s, 0))
    bwd = lambda w: pl.BlockSpec((None, DELTA_BLOCK, w), lambda b, s: (b, ns - 1 - s, 0))
    st = pl.BlockSpec((None, N_CHAIN, B_DK, B_DV), lambda b, s: (b, 0, 0, 0))
    return pl.pallas_call(
        _delta_body,
        grid=(bsz, ns),
        in_specs=[fwd(B_QK), fwd(B_QK), fwd(B_VW), fwd(LANE),
                  pl.BlockSpec((None, N_GATE, DELTA_BLOCK), lambda b, s: (b, 0, s)),
                  bwd(B_QK), bwd(B_QK), bwd(B_VW), bwd(LANE),
                  pl.BlockSpec((None, N_GATE, DELTA_BLOCK), lambda b, s: (b, 0, ns - 1 - s)),
                  st],
        out_specs=[fwd(B_VW), bwd(B_VW), st],
        out_shape=[jax.ShapeDtypeStruct((bsz, t, B_VW), F32), jax.ShapeDtypeStruct((bsz, t, B_VW), F32),
                   jax.ShapeDtypeStruct((bsz, N_CHAIN, B_DK, B_DV), F32)],
        compiler_params=_cparams(("parallel", "arbitrary"), 32),
        name="delta",
    )(bq, bk, bv, gcol, grow, bq, bk, bv, gcol, grow, s0)


def _ab_out_body(x_ref, mod_ref, oa_ref, of_ref, ob_ref, z_ref, gn_ref, w_ref, out_ref):
    mod = mod_ref[...]
    gn = gn_ref[...]
    y = _dot(oa_ref[...], w_ref[0:A_Q, :])
    for hd in range(B_HEADS):
        lo, hi = hd * B_DV, (hd + 1) * B_DV
        o = of_ref[:, lo:hi] + ob_ref[:, lo:hi]
        o = o * lax.rsqrt(jnp.mean(o * o, axis=-1, keepdims=True) + NORM_EPS) * gn
        o = (o * _silu(z_ref[:, lo:hi])).astype(BF16)
        y = y + _dot(o, w_ref[A_Q + lo:A_Q + hi, :])
    out_ref[...] = _res_ln(x_ref[...], y, mod[2:3], mod[3:4], mod[4:5])


def _ab_out(x, mod, oa, of, ob, z, gnorm, w_out, l):
    bsz, t, d = x.shape
    tm = _token_tile(t)
    per_batch = mod.shape[0] > 1
    tok = lambda w: pl.BlockSpec((None, tm, w), lambda b, i: (b, i, 0))
    return pl.pallas_call(
        _ab_out_body,
        grid=(bsz, t // tm),
        in_specs=[tok(d),
                  pl.BlockSpec((None, 8, d), lambda b, i: (b if per_batch else 0, 0, 0)),
                  tok(A_Q), tok(B_VW), tok(B_VW), tok(B_VW),
                  pl.BlockSpec((None, 1, B_DV), lambda b, i: (l, 0, 0)),
                  pl.BlockSpec((None, A_Q + B_VW, d), lambda b, i: (l, 0, 0), pipeline_mode=pl.Buffered(1))],
        out_specs=tok(d),
        out_shape=jax.ShapeDtypeStruct(x.shape, F32),
        compiler_params=_cparams(("parallel", "parallel"), 40),
        name="ab_out",
    )(x, mod, oa, of, ob, z, gnorm, w_out)


def _rope_tables(rows, head_dim):
    n_freq = head_dim // 4
    inv = ROPE_THETA ** (-jnp.arange(n_freq, dtype=F32) / n_freq)
    r, col = jnp.meshgrid(jnp.arange(rows, dtype=F32), jnp.arange(GRID_W, dtype=F32), indexing='ij')
    r, col = r.reshape(-1), col.reshape(-1)
    ang = jnp.concatenate([r[:, None] * inv, col[:, None] * inv], axis=-1)
    cos, sin = jnp.cos(ang), jnp.sin(ang)
    cos2 = jnp.concatenate([cos, cos], axis=-1)
    sin2 = jnp.concatenate([-sin, sin], axis=-1)
    rep = LANE // head_dim
    return jnp.tile(cos2, (1, rep)), jnp.tile(sin2, (1, rep))


def _pad_rows(a, rows):
    return jnp.pad(a, ((0, 0), (0, rows - a.shape[1]), (0, 0)))


def kernel(x, c, ctx, c_ctx, ada_w, ada_b, ln_g, ln_b, ffn_w_gu, ffn_w_down, ab_w_in, ab_conv_w, ab_a_log,
           ab_dt_bias, ab_gnorm, ab_sink, ab_w_out, c_w_in, c_q_norm, c_k_norm, c_w_out):
    bsz, t, d = x.shape
    depth = ada_w.shape[0]
    n_even = ab_w_in.shape[0]
    rows = t // GRID_W
    tab_a = _rope_tables(rows, A_HD)
    tab_c = _rope_tables(rows, C_HD)

    wgu = ffn_w_gu.astype(BF16)
    wd = ffn_w_down.astype(BF16)
    head_cols = jnp.asarray([h * A_HD + j for h in A_HEAD_PERM for j in range(A_HD)], jnp.int32)
    ab_w_in_p = jnp.concatenate([jnp.take(ab_w_in[:, :, :A_Q], head_cols, axis=2), ab_w_in[:, :, A_Q:]], axis=2)
    ab_w_pad = jnp.pad(ab_w_in_p, ((0, 0), (0, 0), (0, AB_IN_PAD - AB_IN))).astype(BF16)
    ab_wo = jnp.concatenate([jnp.take(ab_w_out[:, :A_Q], head_cols, axis=1), ab_w_out[:, A_Q:]], axis=1).astype(BF16)
    c_wi = c_w_in.astype(BF16)
    c_wo = c_w_out.astype(BF16)
    conv_w = _pad_rows(ab_conv_w, 8)
    zeros8 = jnp.zeros((n_even, 2 * B_HEADS), F32)
    a_log16 = jnp.concatenate([zeros8, ab_a_log.reshape(n_even, 2 * B_HEADS)], axis=1)
    dtb16 = jnp.concatenate([zeros8, ab_dt_bias.reshape(n_even, 2 * B_HEADS)], axis=1)
    gate_p = _pad_rows(jnp.pad(jnp.stack([a_log16, dtb16], axis=1), ((0, 0), (0, 0), (0, LANE - N_GATE))), 8)
    sink = jnp.broadcast_to(ab_sink[:, :, None], (n_even, A_HEADS, LANE))
    gnorm = ab_gnorm.reshape(n_even, 1, B_DV)
    qn = c_q_norm.reshape(-1, 1, C_HD)
    kn = c_k_norm.reshape(-1, 1, C_HD)

    nrow = ((bsz + 1 + 7) // 8) * 8
    cvec = jnp.pad(jnp.concatenate([c, c_ctx[None]], axis=0), ((0, nrow - bsz - 1), (0, 0)))
    m = _modulation(cvec, ada_w, ada_b).reshape(depth, nrow, N_MOD, d)

    def mod_rows(l, s):
        ln = jnp.stack([ln_g[l, s], ln_b[l, s]], axis=0)
        rows_l = jnp.concatenate([m[l, :bsz, 3 * s:3 * s + 3], jnp.broadcast_to(ln, (bsz, 2, d)),
                                  jnp.zeros((bsz, 3, d), F32)], axis=1)
        rows_c = jnp.concatenate([m[l, bsz:bsz + 1, 3 * s:3 * s + 3], ln[None],
                                  jnp.zeros((1, 3, d), F32)], axis=1)
        return rows_l, rows_c

    xl, xc = x, ctx
    for l in range(depth):
        ctx_out = l < depth - 1
        i = l // 2
        m0l, m0c = mod_rows(l, 0)
        m1l, m1c = mod_rows(l, 1)
        m2l, m2c = mod_rows(l, 2)
        xl = _ffn(xl, m0l, wgu, wd, l, 0)
        xc = _ffn(xc, m0c, wgu, wd, l, 0)
        if l % 2 == 0:
            pc = _ab_prep(xc, m1c, ab_w_pad, conv_w, gate_p, i, None)
            pl_ = _ab_prep(xl, m1l, ab_w_pad, conv_w, gate_p, i, tab_a)
            aqc, akc, avc, bqc, bkc, bvc, zc, gcc, grc = pc
            aql, akl, avl, bql, bkl, bvl, zl, gcl, grl = pl_
            ol_a = _win_attn(aql, akc, avc, sink, i, akl, avl)
            s0 = jnp.zeros((bsz, N_CHAIN, B_DK, B_DV), F32)
            oc_f, oc_b, s_ctx = _delta(bqc, bkc, bvc, gcc, grc, s0)
            ol_f, ol_b, _ = _delta(bql, bkl, bvl, gcl, grl, s_ctx)
            xl = _ab_out(xl, m1l, ol_a, ol_f, ol_b, zl, gnorm, ab_wo, i)
            if ctx_out:
                oc_a = _win_attn(aqc, akc, avc, sink, i)
                xc = _ab_out(xc, m1c, oc_a, oc_f, oc_b, zc, gnorm, ab_wo, i)
        else:
            qc, kc, vc = _c_prep(xc, m1c, c_wi, i, qn[i], kn[i], None)
            ql, kl, vl = _c_prep(xl, m1l, c_wi, i, qn[i], kn[i], tab_c)
            ol = _flash(ql, [kc, kl], [vc, vl])
            xl = _c_out(xl, m1l, ol, c_wo, i)
            if ctx_out:
                oc = _flash(qc, [kc], [vc])
                xc = _c_out(xc, m1c, oc, c_wo, i)
        xl = _ffn(xl, m2l, wgu, wd, l, 2 - 1)
        if ctx_out:
            xc = _ffn(xc, m2c, wgu, wd, l, 1)
    return xl
```

```python
import functools

import jax
import jax.numpy as jnp
from jax import lax
from jax.experimental import pallas as pl
from jax.experimental.pallas import tpu as pltpu

F32 = jnp.float32
BF16 = jnp.bfloat16

D_MODEL = 1024
DEPTH = 4
GRID_W = 64
D_FF = 2816
N_SUB = 3
N_MOD = 3 * N_SUB
FFN_HALF = 0.5
NORM_EPS = 1e-6
ROPE_THETA = 10000.0
DEEP_ALPHA = (2 * DEPTH) ** 0.25

A_HEADS, A_KV, A_HD = 8, 2, 64
WINDOW = 128
BLOCK = 128
B_HEADS, B_DK, B_DV = 4, 128, 128
CONV_K = 5
CHUNK = 64
C_HEADS, C_KV, C_HD = 8, 2, 128

A_Q = A_HEADS * A_HD
A_KVW = A_KV * A_HD
B_QK = B_HEADS * B_DK
B_VW = B_HEADS * B_DV
B_QKV = 2 * B_QK + B_VW
AB_IN = A_Q + 2 * A_KVW + B_QKV + B_VW + 4 * B_HEADS
N_GATE = 4 * B_HEADS
LANE = 128
AB_IN_PAD = AB_IN - N_GATE + LANE
OFF_AK = A_Q
OFF_AV = A_Q + A_KVW
OFF_BQKV = A_Q + 2 * A_KVW
OFF_BZ = OFF_BQKV + B_QKV
OFF_GATE = OFF_BZ + B_VW
C_QW = C_HEADS * C_HD
C_KW = C_KV * C_HD
C_IN = C_QW + 2 * C_KW
HALO = 8
NEG = -1e30
MIB = 1024 * 1024


def _cparams(sem, vmem_mib):
    return pltpu.CompilerParams(dimension_semantics=sem, vmem_limit_bytes=vmem_mib * MIB)


def _dot(a, b):
    return jnp.dot(a, b, preferred_element_type=F32)


def _dot_nt(a, b):
    return lax.dot_general(a, b, (((1,), (1,)), ((), ())), preferred_element_type=F32)


def _silu(x):
    return x * jax.nn.sigmoid(x)


def _res_ln(x, y, gate, g, b):
    z = DEEP_ALPHA * x + gate * y
    mu = jnp.mean(z, axis=-1, keepdims=True)
    zc = z - mu
    var = jnp.mean(zc * zc, axis=-1, keepdims=True)
    return zc * lax.rsqrt(var + NORM_EPS) * g + b


def _token_tile(t):
    return 512 if t % 512 == 0 else 256


def _mod_body(c_ref, w_ref, b_ref, o_ref):
    c = c_ref[...]
    o_ref[...] = jnp.dot(_silu(c), w_ref[...], preferred_element_type=F32,
                         precision=lax.Precision.HIGHEST) + b_ref[...]


def _modulation(cvec, ada_w, ada_b):
    depth, d, n = ada_w.shape
    rows = cvec.shape[0]
    tn = 2304
    return pl.pallas_call(
        _mod_body,
        grid=(depth, n // tn),
        in_specs=[pl.BlockSpec((rows, d), lambda l, j: (0, 0)),
                  pl.BlockSpec((None, d, tn), lambda l, j: (l, 0, j)),
                  pl.BlockSpec((None, 1, tn), lambda l, j: (l, 0, j))],
        out_specs=pl.BlockSpec((None, rows, tn), lambda l, j: (l, 0, j)),
        out_shape=jax.ShapeDtypeStruct((depth, rows, n), F32),
        compiler_params=_cparams(("parallel", "parallel"), 40),
        name="mod",
    )(cvec, ada_w, ada_b.reshape(depth, 1, n))


FFN_CHUNK = 256


def _ffn_body(x_ref, mod_ref, wgu_ref, wd_ref, o_ref):
    x = x_ref[...]
    mod = mod_ref[...]
    h = (x * (1.0 + mod[1:2]) + mod[0:1]).astype(BF16)
    fc = FFN_CHUNK if x.shape[0] >= 512 else D_FF // 2
    y = None
    for j in range(D_FF // fc):
        gt = _dot(h, wgu_ref[:, j * fc:(j + 1) * fc])
        up = _dot(h, wgu_ref[:, D_FF + j * fc:D_FF + (j + 1) * fc])
        act = (_silu(gt) * up).astype(BF16)
        part = _dot(act, wd_ref[j * fc:(j + 1) * fc, :])
        y = part if y is None else y + part
    o_ref[...] = _res_ln(x, y, FFN_HALF * mod[2:3], mod[3:4], mod[4:5])


def _ffn(x, mod, wgu, wd, l, s):
    bsz, t, d = x.shape
    tm = _token_tile(t)
    per_batch = mod.shape[0] > 1
    return pl.pallas_call(
        _ffn_body,
        grid=(bsz, t // tm),
        in_specs=[pl.BlockSpec((None, tm, d), lambda b, i: (b, i, 0)),
                  pl.BlockSpec((None, 8, d), lambda b, i: (b if per_batch else 0, 0, 0)),
                  pl.BlockSpec((None, None, d, 2 * D_FF), lambda b, i: (l, s, 0, 0),
                               pipeline_mode=pl.Buffered(1)),
                  pl.BlockSpec((None, None, D_FF, d), lambda b, i: (l, s, 0, 0),
                               pipeline_mode=pl.Buffered(1))],
        out_specs=pl.BlockSpec((None, tm, d), lambda b, i: (b, i, 0)),
        out_shape=jax.ShapeDtypeStruct(x.shape, F32),
        compiler_params=_cparams(("parallel", "parallel"), 52),
        name="ffn",
    )(x, mod, wgu, wd)


def _rope_full(x, cos2, sin2):
    return x * cos2 + pltpu.roll(x, C_HD // 2, 1) * sin2


def _c_prep_body(*refs, rope):
    if rope:
        x_ref, mod_ref, w_ref, qn_ref, kn_ref, cos_ref, sin_ref, q_ref, k_ref, v_ref = refs
        cos2, sin2 = cos_ref[...], sin_ref[...]
    else:
        x_ref, mod_ref, w_ref, qn_ref, kn_ref, q_ref, k_ref, v_ref = refs
    x = x_ref[...]
    mod = mod_ref[...]
    h = (x * (1.0 + mod[1:2]) + mod[0:1]).astype(BF16)
    qkv = _dot(h, w_ref[...])
    qn = qn_ref[...] * (C_HD ** -0.5 * LOG2E)
    kn = kn_ref[...]

    heads = [(qkv[:, j * C_HD:(j + 1) * C_HD], qn) for j in range(C_HEADS)]
    heads += [(qkv[:, C_QW + j * C_HD:C_QW + (j + 1) * C_HD], kn) for j in range(C_KV)]
    ms = [jnp.mean(a * a, axis=-1, keepdims=True) for a, _ in heads]
    outs = [a * lax.rsqrt(m + NORM_EPS) * gain for (a, gain), m in zip(heads, ms)]
    if rope:
        outs = [_rope_full(a, cos2, sin2) for a in outs]
    for j in range(C_HEADS):
        q_ref[:, j * C_HD:(j + 1) * C_HD] = outs[j].astype(BF16)
    for j in range(C_KV):
        k_ref[:, j * C_HD:(j + 1) * C_HD] = outs[C_HEADS + j].astype(BF16)
    v_ref[...] = qkv[:, C_QW + C_KW:].astype(BF16)


def _c_prep(x, mod, w_in, l, qn, kn, tables):
    bsz, t, d = x.shape
    tm = _token_tile(t)
    per_batch = mod.shape[0] > 1
    rope = tables is not None
    in_specs = [pl.BlockSpec((None, tm, d), lambda b, i: (b, i, 0)),
                pl.BlockSpec((None, 8, d), lambda b, i: (b if per_batch else 0, 0, 0)),
                pl.BlockSpec((None, d, C_IN), lambda b, i: (l, 0, 0), pipeline_mode=pl.Buffered(1)),
                pl.BlockSpec((1, C_HD), lambda b, i: (0, 0)),
                pl.BlockSpec((1, C_HD), lambda b, i: (0, 0))]
    args = [x, mod, w_in, qn, kn]
    if rope:
        in_specs += [pl.BlockSpec((tm, C_HD), lambda b, i: (i, 0))] * 2
        args += list(tables)
    return pl.pallas_call(
        functools.partial(_c_prep_body, rope=rope),
        grid=(bsz, t // tm),
        in_specs=in_specs,
        out_specs=[pl.BlockSpec((None, tm, C_QW), lambda b, i: (b, i, 0)),
                   pl.BlockSpec((None, tm, C_KW), lambda b, i: (b, i, 0)),
                   pl.BlockSpec((None, tm, C_KW), lambda b, i: (b, i, 0))],
        out_shape=[jax.ShapeDtypeStruct((bsz, t, C_QW), BF16),
                   jax.ShapeDtypeStruct((bsz, t, C_KW), BF16),
                   jax.ShapeDtypeStruct((bsz, t, C_KW), BF16)],
        compiler_params=_cparams(("parallel", "parallel"), 40),
        name="c_prep",
    )(*args)


C_GROUP = C_HEADS // C_KV
TQ_C = 128
KC_C = 256
RB_C = 64
FLASH_UNROLL = 4
LOG2E = 1.4426950408889634


def _flash_body(q_ref, k_ref, v_ref, o_ref, q_scr, s_scr, p0_scr, p1_scr, m_scr, mb_scr, acc_scr, *,
                nch, tail_keys):
    i = pl.program_id(2)
    slot_a = i % 2
    slot_b = 1 - slot_a
    rows = q_scr.shape[0]
    tq = q_ref.shape[0]

    @pl.when(i == 0)
    def _():
        s_scr[1] = jnp.zeros(s_scr.shape[1:], F32)
        m_scr[1] = jnp.zeros(m_scr.shape[1:], F32)

    for g in range(C_GROUP):
        q_scr[g * tq:(g + 1) * tq, :] = q_ref[:, g * C_HD:(g + 1) * C_HD]
    m_cur = jnp.max(m_scr[slot_b], axis=-1, keepdims=True)
    mb_scr[...] = jnp.broadcast_to(m_cur, (rows, LANE))
    m_scr[slot_a] = jnp.full((rows, LANE), NEG, F32)
    acc_scr[...] = jnp.zeros((rows, 2 * C_HD), F32)

    def probs(j, p_scr):
        for r0 in range(0, rows, RB_C):
            mb = mb_scr[r0:r0 + RB_C, :]
            for c0 in range(0, KC_C, LANE):
                p = jnp.exp2(s_scr[slot_b, j, r0:r0 + RB_C, c0:c0 + LANE] - mb)
                p_scr[r0:r0 + RB_C, c0:c0 + LANE] = p.astype(BF16)

    def pv(j, p_scr):
        k0 = pl.multiple_of(j * KC_C, KC_C)
        acc_scr[...] += _dot(p_scr[...], v_ref[pl.ds(k0, KC_C), :])

    def logits(j, n_keys=KC_C):
        k0 = pl.multiple_of(j * KC_C, KC_C)
        s = _dot_nt(q_scr[...], k_ref[pl.ds(k0, KC_C), :])
        if n_keys < KC_C:
            s = jnp.where(lax.broadcasted_iota(jnp.int32, (1, KC_C), 1) < n_keys, s, NEG)
        s_scr[slot_a, j] = s
        mt = s[:, 0:LANE]
        for c0 in range(LANE, KC_C, LANE):
            mt = jnp.maximum(mt, s[:, c0:c0 + LANE])
        m_scr[slot_a] = jnp.maximum(m_scr[slot_a], mt)

    probs(0, p0_scr)

    def pair(t, carry):
        j = 2 * t
        logits(j)
        pv(j, p0_scr)
        probs(j + 1, p1_scr)
        logits(j + 1)
        pv(j + 1, p1_scr)
        probs(j + 2, p0_scr)
        return carry

    lax.fori_loop(0, (nch - 1) // 2, pair, 0, unroll=FLASH_UNROLL)
    pv(nch - 1, p0_scr)
    logits(nch - 1, tail_keys)

    acc = acc_scr[...]
    o = acc[:, :C_HD] / acc[:, C_HD:C_HD + 1]
    for g in range(C_GROUP):
        o_ref[:, g * C_HD:(g + 1) * C_HD] = o[g * tq:(g + 1) * tq].astype(BF16)


def _flash(q, ks, vs):
    bsz, t, _ = q.shape
    n_keys = sum(a.shape[1] for a in ks)
    nch = -(-n_keys // KC_C)
    nch += 1 - nch % 2
    n = nch * KC_C
    tail_keys = n_keys - (nch - 1) * KC_C
    assert 0 < tail_keys <= KC_C
    pad = [jnp.zeros((bsz, n - n_keys, C_KW), BF16)] if n > n_keys else []
    k = jnp.concatenate(list(ks) + pad, axis=1)
    v = jnp.concatenate(list(vs) + pad, axis=1)
    ones = jnp.zeros((bsz, n, C_KV, C_HD), BF16).at[..., 0].set(1.0)
    v = jnp.concatenate([v.reshape(bsz, n, C_KV, C_HD), ones], axis=-1).reshape(bsz, n, 2 * C_KW)
    tq = TQ_C
    nq = t // tq
    rows = C_GROUP * tq
    gw = C_GROUP * C_HD
    return pl.pallas_call(
        functools.partial(_flash_body, nch=nch, tail_keys=tail_keys),
        grid=(bsz, C_KV, nq + 1),
        in_specs=[pl.BlockSpec((None, tq, gw), lambda b, h, i: (b, jnp.minimum(i, nq - 1), h)),
                  pl.BlockSpec((None, n, C_HD), lambda b, h, i: (b, 0, h)),
                  pl.BlockSpec((None, n, 2 * C_HD), lambda b, h, i: (b, 0, h))],
        out_specs=pl.BlockSpec((None, tq, gw), lambda b, h, i: (b, jnp.maximum(i - 1, 0), h)),
        out_shape=jax.ShapeDtypeStruct((bsz, t, C_QW), BF16),
        scratch_shapes=[pltpu.VMEM((rows, C_HD), BF16),
                        pltpu.VMEM((2, nch, rows, KC_C), F32),
                        pltpu.VMEM((rows, KC_C), BF16),
                        pltpu.VMEM((rows, KC_C), BF16),
                        pltpu.VMEM((2, rows, LANE), F32),
                        pltpu.VMEM((rows, LANE), F32),
                        pltpu.VMEM((rows, 2 * C_HD), F32)],
        compiler_params=_cparams(("parallel", "parallel", "arbitrary"), 48),
        name="flash",
    )(q, k, v)


def _c_out_body(x_ref, mod_ref, o_ref, w_ref, out_ref):
    mod = mod_ref[...]
    y = _dot(o_ref[...], w_ref[...])
    out_ref[...] = _res_ln(x_ref[...], y, mod[2:3], mod[3:4], mod[4:5])


def _c_out(x, mod, o, w_out, l):
    bsz, t, d = x.shape
    tm = _token_tile(t)
    per_batch = mod.shape[0] > 1
    return pl.pallas_call(
        _c_out_body,
        grid=(bsz, t // tm),
        in_specs=[pl.BlockSpec((None, tm, d), lambda b, i: (b, i, 0)),
                  pl.BlockSpec((None, 8, d), lambda b, i: (b if per_batch else 0, 0, 0)),
                  pl.BlockSpec((None, tm, C_QW), lambda b, i: (b, i, 0)),
                  pl.BlockSpec((None, C_QW, d), lambda b, i: (l, 0, 0), pipeline_mode=pl.Buffered(1))],
        out_specs=pl.BlockSpec((None, tm, d), lambda b, i: (b, i, 0)),
        out_shape=jax.ShapeDtypeStruct(x.shape, F32),
        compiler_params=_cparams(("parallel", "parallel"), 32),
        name="c_out",
    )(x, mod, o, w_out)


def _rope_half_tile(x, cos2, sin2, lane_lo):
    half = A_HD // 2
    partner = jnp.where(lane_lo, pltpu.roll(x, LANE - half, 1), pltpu.roll(x, half, 1))
    return x * cos2 + partner * sin2


def _gate_act(raw, is_beta, a_log, dt_bias):
    sp_in = raw + dt_bias
    softplus = jnp.maximum(sp_in, 0.0) + jnp.log(1.0 + jnp.exp(-jnp.abs(sp_in)))
    return jnp.where(is_beta, jax.nn.sigmoid(raw), -jnp.exp(a_log) * softplus)


def _ab_prep_body(*refs, rope, tm):
    if rope:
        (xp_ref, x_ref, xn_ref, mod_ref, w_ref, cw_ref, gp_ref, cos_ref, sin_ref,
         aq_ref, ak_ref, av_ref, bq_ref, bk_ref, bv_ref, z_ref, gc_ref, gr_ref, cbuf) = refs
    else:
        (xp_ref, x_ref, xn_ref, mod_ref, w_ref, cw_ref, gp_ref,
         aq_ref, ak_ref, av_ref, bq_ref, bk_ref, bv_ref, z_ref, gc_ref, gr_ref, cbuf) = refs
    i = pl.program_id(1)
    nt = pl.num_programs(1)
    mod = mod_ref[...]
    scale1, shift = 1.0 + mod[1:2], mod[0:1]
    h = (x_ref[...] * scale1 + shift).astype(BF16)
    halo = jnp.concatenate([xp_ref[...], xn_ref[...]], axis=0)
    h_all = jnp.concatenate([h, (halo * scale1 + shift).astype(BF16)], axis=0)
    proj_all = _dot(h_all, w_ref[...])
    proj = proj_all[:tm]

    if rope:
        cos2, sin2 = cos_ref[...], sin_ref[...]
        lane_lo = (lax.broadcasted_iota(jnp.int32, (tm, LANE), 1) % A_HD) < (A_HD // 2)
    for j in range(A_Q // LANE):
        a = proj[:, j * LANE:(j + 1) * LANE]
        if rope:
            a = _rope_half_tile(a, cos2, sin2, lane_lo)
        aq_ref[:, j * LANE:(j + 1) * LANE] = (a * (A_HD ** -0.5 * LOG2E)).astype(BF16)
    a = proj[:, OFF_AK:OFF_AK + LANE]
    if rope:
        a = _rope_half_tile(a, cos2, sin2, lane_lo)
    ak_ref[...] = a.astype(BF16)
    av_ref[...] = proj[:, OFF_AV:OFF_AV + LANE].astype(BF16)
    z_ref[...] = proj[:, OFF_BZ:OFF_BZ + B_VW]

    gp = gp_ref[...]
    lane = lax.broadcasted_iota(jnp.int32, (1, LANE), 1)
    gates = _gate_act(proj[:, OFF_GATE:OFF_GATE + LANE], lane < 2 * B_HEADS, gp[0:1], gp[1:2])
    gc_ref[...] = gates
    gr_ref[...] = gates.T[:N_GATE, :]

    cbuf[0:HALO, :] = proj_all[tm:tm + HALO, OFF_BQKV:OFF_BQKV + B_QKV] * jnp.where(i > 0, 1.0, 0.0)
    cbuf[HALO:HALO + tm, :] = proj[:, OFF_BQKV:OFF_BQKV + B_QKV]
    cbuf[HALO + tm:, :] = proj_all[tm + HALO:, OFF_BQKV:OFF_BQKV + B_QKV] * jnp.where(i < nt - 1, 1.0, 0.0)
    cw = cw_ref[...]
    pad = CONV_K // 2
    for part, dst in enumerate((bq_ref, bk_ref, bv_ref)):
        for hd in range(B_HEADS):
            c0 = part * B_QK + hd * B_DK
            acc = None
            for j in range(CONV_K):
                term = cbuf[HALO - pad + j:HALO - pad + j + tm, c0:c0 + B_DK] * cw[j:j + 1, c0:c0 + B_DK]
                acc = term if acc is None else acc + term
            acc = _silu(acc)
            if part < 2:
                acc = acc * lax.rsqrt(jnp.sum(acc * acc, axis=-1, keepdims=True) + NORM_EPS)
            if part == 0:
                acc = acc * (B_DK ** -0.5)
            dst[:, hd * B_DK:(hd + 1) * B_DK] = acc


def _ab_prep(x, mod, w_pad, conv_w, gate_p, l, tables):
    bsz, t, d = x.shape
    tm = _token_tile(t)
    nh = tm // HALO
    nblk = t // HALO
    per_batch = mod.shape[0] > 1
    rope = tables is not None
    const = dict(pipeline_mode=pl.Buffered(1))
    in_specs = [pl.BlockSpec((None, HALO, d), lambda b, i: (b, jnp.maximum(i * nh - 1, 0), 0)),
                pl.BlockSpec((None, tm, d), lambda b, i: (b, i, 0)),
                pl.BlockSpec((None, HALO, d), lambda b, i: (b, jnp.minimum((i + 1) * nh, nblk - 1), 0)),
                pl.BlockSpec((None, 8, d), lambda b, i: (b if per_batch else 0, 0, 0)),
                pl.BlockSpec((None, d, AB_IN_PAD), lambda b, i: (l, 0, 0), **const),
                pl.BlockSpec((None, 8, B_QKV), lambda b, i: (l, 0, 0)),
                pl.BlockSpec((None, 8, LANE), lambda b, i: (l, 0, 0))]
    args = [x, x, x, mod, w_pad, conv_w, gate_p]
    if rope:
        in_specs += [pl.BlockSpec((tm, LANE), lambda b, i: (i, 0))] * 2
        args += list(tables)
    tok = lambda w: pl.BlockSpec((None, tm, w), lambda b, i: (b, i, 0))
    shp = lambda w, dt: jax.ShapeDtypeStruct((bsz, t, w), dt)
    return pl.pallas_call(
        functools.partial(_ab_prep_body, rope=rope, tm=tm),
        grid=(bsz, t // tm),
        in_specs=in_specs,
        out_specs=[tok(A_Q), tok(A_KVW), tok(A_KVW), tok(B_QK), tok(B_QK), tok(B_VW), tok(B_VW), tok(LANE),
                   pl.BlockSpec((None, N_GATE, tm), lambda b, i: (b, 0, i))],
        out_shape=[shp(A_Q, BF16), shp(A_KVW, BF16), shp(A_KVW, BF16), shp(B_QK, F32), shp(B_QK, F32),
                   shp(B_VW, F32), shp(B_VW, F32), shp(LANE, F32),
                   jax.ShapeDtypeStruct((bsz, N_GATE, t), F32)],
        scratch_shapes=[pltpu.VMEM((tm + 2 * HALO, B_QKV), F32)],
        compiler_params=_cparams(("parallel", "parallel"), 52),
        name="ab_prep",
    )(*args)


A_GROUP = A_HEADS // A_KV


WIN_TILE = 1024
WIN_GROUP = 1
A_HEAD_PERM = tuple(h for t in range(A_GROUP) for h in (t, A_GROUP + t))


def _win_body(*refs, local, nqb):
    if local:
        q_ref, kp_ref, kc_ref, kn_ref, vp_ref, vc_ref, vn_ref, kx_ref, vx_ref, sink_ref, o_ref = refs
    else:
        q_ref, kx_ref, vx_ref, sink_ref, o_ref = refs
    lane = lax.broadcasted_iota(jnp.int32, (1, LANE), 1)
    kv_lanes = (lane < A_HD, lane >= A_HD)
    kx = kx_ref[...]
    vx = vx_ref[...]
    sink = sink_ref[...] * LOG2E
    if local:
        i = pl.program_id(1)
        nsteps = pl.num_programs(1)
        kcat = jnp.concatenate([kp_ref[...], kc_ref[...], kn_ref[...]], axis=0)
        vcat = jnp.concatenate([vp_ref[...], vc_ref[...], vn_ref[...]], axis=0)
        r = lax.broadcasted_iota(jnp.int32, (BLOCK, 3 * BLOCK), 0)
        c = lax.broadcasted_iota(jnp.int32, (BLOCK, 3 * BLOCK), 1)
        band = (c >= r) & (c <= r + 2 * WINDOW)
    zero = jnp.zeros((), BF16)
    sks = [jnp.concatenate([jnp.broadcast_to(sink[kv * A_GROUP + g:kv * A_GROUP + g + 1, 0:1], (BLOCK, 1))
                            for g in range(A_GROUP)], axis=0) for kv in range(A_KV)]
    def logits(qb0):
        units = []
        for qb in range(qb0, min(qb0 + WIN_GROUP, nqb)):
            r0 = qb * BLOCK
            tiles = [q_ref[r0:r0 + BLOCK, t * LANE:(t + 1) * LANE] for t in range(A_GROUP)]
            for kv in range(A_KV):
                qs = jnp.concatenate([jnp.where(kv_lanes[kv], tl, zero) for tl in tiles], axis=0)
                units.append(dict(qb=qb, kv=kv, r0=r0, qs=qs))
        for un in units:
            un["s_x"] = _dot_nt(un["qs"], kx)
            if local:
                un["s_l"] = _dot_nt(un["qs"], kcat[un["r0"]:un["r0"] + 3 * BLOCK])
        return units

    def finish(units):
        for un in units:
            m = jnp.maximum(jnp.max(un["s_x"], axis=-1, keepdims=True), sks[un["kv"]])
            if local:
                valid = band
                if un["qb"] == 0:
                    valid = valid & ((c >= BLOCK) | (i > 0))
                if un["qb"] == nqb - 1:
                    valid = valid & ((c < 2 * BLOCK) | (i < nsteps - 1))
                un["s_l"] = jnp.concatenate(
                    [jnp.where(valid, un["s_l"][g * BLOCK:(g + 1) * BLOCK], NEG) for g in range(A_GROUP)], axis=0)
                m = jnp.maximum(m, jnp.max(un["s_l"], axis=-1, keepdims=True))
            un["m"] = m
        for un in units:
            p_x = jnp.exp2(un["s_x"] - un["m"])
            un["den"] = jnp.sum(p_x, axis=-1, keepdims=True) + jnp.exp2(sks[un["kv"]] - un["m"])
            un["p_x"] = p_x.astype(BF16)
            if local:
                p_l = jnp.exp2(un["s_l"] - un["m"])
                un["den"] = un["den"] + jnp.sum(p_l, axis=-1, keepdims=True)
                un["p_l"] = p_l.astype(BF16)
        for un in units:
            o = _dot(un["p_x"], vx)
            if local:
                o = o + _dot(un["p_l"], vcat[un["r0"]:un["r0"] + 3 * BLOCK])
            un["o"] = o / un["den"]
        for u0 in range(0, len(units), A_KV):
            r0 = units[u0]["r0"]
            for t in range(A_GROUP):
                o_ref[r0:r0 + BLOCK, t * LANE:(t + 1) * LANE] = jnp.where(
                    kv_lanes[0], units[u0]["o"][t * BLOCK:(t + 1) * BLOCK],
                    units[u0 + 1]["o"][t * BLOCK:(t + 1) * BLOCK]).astype(BF16)

    starts = list(range(0, nqb, WIN_GROUP))
    pending = logits(starts[0])
    for nxt in starts[1:] + [None]:
        ahead = logits(nxt) if nxt is not None else None
        finish(pending)
        pending = ahead


def _win_attn(q, kx, vx, sink, l, k=None, v=None):
    bsz, t, _ = q.shape
    lc = kx.shape[1]
    local = k is not None
    tq = min(WIN_TILE, t)
    nqb = tq // BLOCK
    nb = t // BLOCK
    edge = lambda f: pl.BlockSpec((None, BLOCK, A_KVW), f)
    in_specs = [pl.BlockSpec((None, tq, A_Q), lambda b, i: (b, i, 0))]
    args = [q]
    if local:
        prv = lambda b, i: (b, jnp.maximum(i * nqb - 1, 0), 0)
        nxt = lambda b, i: (b, jnp.minimum((i + 1) * nqb, nb - 1), 0)
        cur = pl.BlockSpec((None, tq, A_KVW), lambda b, i: (b, i, 0))
        in_specs += [edge(prv), cur, edge(nxt), edge(prv), cur, edge(nxt)]
        args += [k, k, k, v, v, v]
    in_specs += [pl.BlockSpec((None, lc, A_KVW), lambda b, i: (b, 0, 0)),
                 pl.BlockSpec((None, lc, A_KVW), lambda b, i: (b, 0, 0)),
                 pl.BlockSpec((None, A_HEADS, LANE), lambda b, i: (l, 0, 0))]
    args += [kx, vx, sink]
    return pl.pallas_call(
        functools.partial(_win_body, local=local, nqb=nqb),
        grid=(bsz, t // tq),
        in_specs=in_specs,
        out_specs=pl.BlockSpec((None, tq, A_Q), lambda b, i: (b, i, 0)),
        out_shape=jax.ShapeDtypeStruct((bsz, t, A_Q), BF16),
        compiler_params=_cparams(("parallel", "parallel"), 40),
        name="win_attn",
    )(*args)


DELTA_BLOCK = 4 * CHUNK
DELTA_CHUNKS = DELTA_BLOCK // CHUNK
N_CHAIN = 2 * B_HEADS


def _split_bf16(a):
    hi = a.astype(BF16)
    return hi, (a - hi.astype(F32)).astype(BF16)


def _delta_intra(units, eye):
    for un in units:
        decay = jnp.exp(jnp.where(un["incl"], un["gcc"] - un["gcr"], NEG))
        kb = un["k"] * un["beta"]
        a = _dot_nt(jnp.concatenate([kb, un["q"]], axis=0).astype(BF16), un["k"].astype(BF16))
        un["p"] = jnp.where(un["strict"], a[:CHUNK] * decay, 0.0) * -1.0
        un["attn"] = (a[CHUNK:] * decay).astype(BF16)
        un["t"] = eye + un["p"]
        eg = jnp.exp(un["gcc"])
        un["rhs"] = jnp.concatenate([un["v"] * un["beta"], kb * eg], axis=1).astype(BF16)
        un["qd"] = un["q"] * eg
        un["kt_t"] = (un["k"] * jnp.exp(un["gtot"] - un["gcc"])).T.astype(BF16)
        un["dec"] = jnp.exp(un["gtot"])
    for un in units:
        pb = un["p"].astype(BF16)
        un["pb"] = _dot(pb, pb).astype(BF16)
    for _ in range(CHUNK.bit_length() - 3):
        for un in units:
            both = _dot(jnp.concatenate([un["pb"], un["t"].astype(BF16)], axis=0), un["pb"])
            un["t"] = un["t"] + both[CHUNK:]
            un["pb"] = both[:CHUNK].astype(BF16)
    for un in units:
        un["t"] = un["t"] + _dot(un["t"].astype(BF16), un["pb"])
    for un in units:
        uw = _dot(un["t"].astype(BF16), un["rhs"])
        un["u"] = uw[:, :B_DV]
        un["wq"] = jnp.concatenate([uw[:, B_DV:], un["qd"]], axis=0).astype(BF16)


def _delta_body(qf_ref, kf_ref, vf_ref, gcf_ref, grf_ref, qb_ref, kb_ref, vb_ref, gcb_ref, grb_ref, s0_ref,
                of_ref, ob_ref, s_ref):
    @pl.when(pl.program_id(1) == 0)
    def _():
        s_ref[...] = s0_ref[...]

    li = lax.broadcasted_iota(jnp.int32, (CHUNK, CHUNK), 0)
    lj = lax.broadcasted_iota(jnp.int32, (CHUNK, CHUNK), 1)
    eye = jnp.where(li == lj, 1.0, 0.0)
    tri_l = jnp.where(li >= lj, 1.0, 0.0).astype(BF16)
    tri_u = jnp.where(li <= lj, 1.0, 0.0).astype(BF16)

    dirs = ((qf_ref, kf_ref, vf_ref, gcf_ref, grf_ref, of_ref, tuple(range(DELTA_CHUNKS)), tri_l, tri_u, li >= lj, li > lj, CHUNK - 1),
            (qb_ref, kb_ref, vb_ref, gcb_ref, grb_ref, ob_ref, tuple(reversed(range(DELTA_CHUNKS))), tri_u, tri_l, li <= lj, li < lj, 0))
    units = {}
    for d, (q_ref, k_ref, v_ref, gc_ref, gr_ref, o_ref, order, tri_c, tri_r, incl, strict, last) in enumerate(dirs):
        for c in order:
            r0 = c * CHUNK
            gcol = gc_ref[r0:r0 + CHUNK, :]
            ghi, glo = _split_bf16(gcol)
            cum_c = _dot(tri_c, ghi) + _dot(tri_c, glo)
            grow = gr_ref[:, r0:r0 + CHUNK]
            rhi, rlo = _split_bf16(grow)
            cum_r = _dot(rhi, tri_r) + _dot(rlo, tri_r)
            for hd in range(B_HEADS):
                col = d * B_HEADS + hd
                gl = 2 * B_HEADS + col
                lo, hi = hd * B_DK, (hd + 1) * B_DK
                units[(d, c, hd)] = dict(
                    q=q_ref[r0:r0 + CHUNK, lo:hi], k=k_ref[r0:r0 + CHUNK, lo:hi], v=v_ref[r0:r0 + CHUNK, lo:hi],
                    beta=gcol[:, col:col + 1], gcc=cum_c[:, gl:gl + 1], gcr=cum_r[gl:gl + 1, :],
                    gtot=cum_c[last:last + 1, gl:gl + 1], incl=incl, strict=strict)
    _delta_intra(list(units.values()), eye)

    chains = [(d, hd) for d in range(2) for hd in range(B_HEADS)]
    state = [s_ref[d * B_HEADS + hd] for d, hd in chains]
    for step in range(DELTA_CHUNKS):
        cur = [units[(d, dirs[d][6][step], hd)] for d, hd in chains]
        ws = [_dot(un["wq"], s.astype(BF16)) for un, s in zip(cur, state)]
        v_new = [(un["u"] - w[:CHUNK]).astype(BF16) for un, w in zip(cur, ws)]
        outs = [w[CHUNK:] + _dot(un["attn"], vn) for un, w, vn in zip(cur, ws, v_new)]
        state = [s * un["dec"] + _dot(un["kt_t"], vn) for un, s, vn in zip(cur, state, v_new)]
        for (d, hd), o in zip(chains, outs):
            r0 = dirs[d][6][step] * CHUNK
            dirs[d][5][r0:r0 + CHUNK, hd * B_DV:(hd + 1) * B_DV] = o
    for (d, hd), s in zip(chains, state):
        s_ref[d * B_HEADS + hd] = s


def _delta(bq, bk, bv, gcol, grow, s0):
    bsz, t, _ = bq.shape
    ns = t // DELTA_BLOCK
    fwd = lambda w: pl.BlockSpec((None, DELTA_BLOCK, w), lambda b, s: (b, s, 0))
    bwd = lambda w: pl.BlockSpec((None, DELTA_BLOCK, w), lambda b, s: (b, ns - 1 - s, 0))
    st = pl.BlockSpec((None, N_CHAIN, B_DK, B_DV), lambda b, s: (b, 0, 0, 0))
    return pl.pallas_call(
        _delta_body,
        grid=(bsz, ns),
        in_specs=[fwd(B_QK), fwd(B_QK), fwd(B_VW), fwd(LANE),
                  pl.BlockSpec((None, N_GATE, DELTA_BLOCK), lambda b, s: (b, 0, s)),
                  bwd(B_QK), bwd(B_QK), bwd(B_VW), bwd(LANE),
                  pl.BlockSpec((None, N_GATE, DELTA_BLOCK), lambda b, s: (b, 0, ns - 1 - s)),
                  st],
        out_specs=[fwd(B_VW), bwd(B_VW), st],
        out_shape=[jax.ShapeDtypeStruct((bsz, t, B_VW), F32), jax.ShapeDtypeStruct((bsz, t, B_VW), F32),
                   jax.ShapeDtypeStruct((bsz, N_CHAIN, B_DK, B_DV), F32)],
        compiler_params=_cparams(("parallel", "arbitrary"), 32),
        name="delta",
    )(bq, bk, bv, gcol, grow, bq, bk, bv, gcol, grow, s0)


def _ab_out_body(x_ref, mod_ref, oa_ref, of_ref, ob_ref, z_ref, gn_ref, w_ref, out_ref):
    mod = mod_ref[...]
    gn = gn_ref[...]
    y = _dot(oa_ref[...], w_ref[0:A_Q, :])
    for hd in range(B_HEADS):
        lo, hi = hd * B_DV, (hd + 1) * B_DV
        o = of_ref[:, lo:hi] + ob_ref[:, lo:hi]
        o = o * lax.rsqrt(jnp.mean(o * o, axis=-1, keepdims=True) + NORM_EPS) * gn
        o = (o * _silu(z_ref[:, lo:hi])).astype(BF16)
        y = y + _dot(o, w_ref[A_Q + lo:A_Q + hi, :])
    out_ref[...] = _res_ln(x_ref[...], y, mod[2:3], mod[3:4], mod[4:5])


def _ab_out(x, mod, oa, of, ob, z, gnorm, w_out, l):
    bsz, t, d = x.shape
    tm = _token_tile(t)
    per_batch = mod.shape[0] > 1
    tok = lambda w: pl.BlockSpec((None, tm, w), lambda b, i: (b, i, 0))
    return pl.pallas_call(
        _ab_out_body,
        grid=(bsz, t // tm),
        in_specs=[tok(d),
                  pl.BlockSpec((None, 8, d), lambda b, i: (b if per_batch else 0, 0, 0)),
                  tok(A_Q), tok(B_VW), tok(B_VW), tok(B_VW),
                  pl.BlockSpec((None, 1, B_DV), lambda b, i: (l, 0, 0)),
                  pl.BlockSpec((None, A_Q + B_VW, d), lambda b, i: (l, 0, 0), pipeline_mode=pl.Buffered(1))],
        out_specs=tok(d),
        out_shape=jax.ShapeDtypeStruct(x.shape, F32),
        compiler_params=_cparams(("parallel", "parallel"), 40),
        name="ab_out",
    )(x, mod, oa, of, ob, z, gnorm, w_out)


def _rope_tables(rows, head_dim):
    n_freq = head_dim // 4
    inv = ROPE_THETA ** (-jnp.arange(n_freq, dtype=F32) / n_freq)
    r, col = jnp.meshgrid(jnp.arange(rows, dtype=F32), jnp.arange(GRID_W, dtype=F32), indexing='ij')
    r, col = r.reshape(-1), col.reshape(-1)
    ang = jnp.concatenate([r[:, None] * inv, col[:, None] * inv], axis=-1)
    cos, sin = jnp.cos(ang), jnp.sin(ang)
    cos2 = jnp.concatenate([cos, cos], axis=-1)
    sin2 = jnp.concatenate([-sin, sin], axis=-1)
    rep = LANE // head_dim
    return jnp.tile(cos2, (1, rep)), jnp.tile(sin2, (1, rep))


def _pad_rows(a, rows):
    return jnp.pad(a, ((0, 0), (0, rows - a.shape[1]), (0, 0)))


def kernel(x, c, ctx, c_ctx, ada_w, ada_b, ln_g, ln_b, ffn_w_gu, ffn_w_down, ab_w_in, ab_conv_w, ab_a_log,
           ab_dt_bias, ab_gnorm, ab_sink, ab_w_out, c_w_in, c_q_norm, c_k_norm, c_w_out):
    bsz, t, d = x.shape
    depth = ada_w.shape[0]
    n_even = ab_w_in.shape[0]
    rows = t // GRID_W
    tab_a = _rope_tables(rows, A_HD)
    tab_c = _rope_tables(rows, C_HD)

    wgu = ffn_w_gu.astype(BF16)
    wd = ffn_w_down.astype(BF16)
    head_cols = jnp.asarray([h * A_HD + j for h in A_HEAD_PERM for j in range(A_HD)], jnp.int32)
    ab_w_in_p = jnp.concatenate([jnp.take(ab_w_in[:, :, :A_Q], head_cols, axis=2), ab_w_in[:, :, A_Q:]], axis=2)
    ab_w_pad = jnp.pad(ab_w_in_p, ((0, 0), (0, 0), (0, AB_IN_PAD - AB_IN))).astype(BF16)
    ab_wo = jnp.concatenate([jnp.take(ab_w_out[:, :A_Q], head_cols, axis=1), ab_w_out[:, A_Q:]], axis=1).astype(BF16)
    c_wi = c_w_in.astype(BF16)
    c_wo = c_w_out.astype(BF16)
    conv_w = _pad_rows(ab_conv_w, 8)
    zeros8 = jnp.zeros((n_even, 2 * B_HEADS), F32)
    a_log16 = jnp.concatenate([zeros8, ab_a_log.reshape(n_even, 2 * B_HEADS)], axis=1)
    dtb16 = jnp.concatenate([zeros8, ab_dt_bias.reshape(n_even, 2 * B_HEADS)], axis=1)
    gate_p = _pad_rows(jnp.pad(jnp.stack([a_log16, dtb16], axis=1), ((0, 0), (0, 0), (0, LANE - N_GATE))), 8)
    sink = jnp.broadcast_to(ab_sink[:, :, None], (n_even, A_HEADS, LANE))
    gnorm = ab_gnorm.reshape(n_even, 1, B_DV)
    qn = c_q_norm.reshape(-1, 1, C_HD)
    kn = c_k_norm.reshape(-1, 1, C_HD)

    nrow = ((bsz + 1 + 7) // 8) * 8
    cvec = jnp.pad(jnp.concatenate([c, c_ctx[None]], axis=0), ((0, nrow - bsz - 1), (0, 0)))
    m = _modulation(cvec, ada_w, ada_b).reshape(depth, nrow, N_MOD, d)

    def mod_rows(l, s):
        ln = jnp.stack([ln_g[l, s], ln_b[l, s]], axis=0)
        rows_l = jnp.concatenate([m[l, :bsz, 3 * s:3 * s + 3], jnp.broadcast_to(ln, (bsz, 2, d)),
                                  jnp.zeros((bsz, 3, d), F32)], axis=1)
        rows_c = jnp.concatenate([m[l, bsz:bsz + 1, 3 * s:3 * s + 3], ln[None],
                                  jnp.zeros((1, 3, d), F32)], axis=1)
        return rows_l, rows_c

    xl, xc = x, ctx
    for l in range(depth):
        ctx_out = l < depth - 1
        i = l // 2
        m0l, m0c = mod_rows(l, 0)
        m1l, m1c = mod_rows(l, 1)
        m2l, m2c = mod_rows(l, 2)
        xl = _ffn(xl, m0l, wgu, wd, l, 0)
        xc = _ffn(xc, m0c, wgu, wd, l, 0)
        if l % 2 == 0:
            pc = _ab_prep(xc, m1c, ab_w_pad, conv_w, gate_p, i, None)
            pl_ = _ab_prep(xl, m1l, ab_w_pad, conv_w, gate_p, i, tab_a)
            aqc, akc, avc, bqc, bkc, bvc, zc, gcc, grc = pc
            aql, akl, avl, bql, bkl, bvl, zl, gcl, grl = pl_
            ol_a = _win_attn(aql, akc, avc, sink, i, akl, avl)
            s0 = jnp.zeros((bsz, N_CHAIN, B_DK, B_DV), F32)
            oc_f, oc_b, s_ctx = _delta(bqc, bkc, bvc, gcc, grc, s0)
            ol_f, ol_b, _ = _delta(bql, bkl, bvl, gcl, grl, s_ctx)
            xl = _ab_out(xl, m1l, ol_a, ol_f, ol_b, zl, gnorm, ab_wo, i)
            if ctx_out:
                oc_a = _win_attn(aqc, akc, avc, sink, i)
                xc = _ab_out(xc, m1c, oc_a, oc_f, oc_b, zc, gnorm, ab_wo, i)
        else:
            qc, kc, vc = _c_prep(xc, m1c, c_wi, i, qn[i], kn[i], None)
            ql, kl, vl = _c_prep(xl, m1l, c_wi, i, qn[i], kn[i], tab_c)
            ol = _flash(ql, [kc, kl], [vc, vl])
            xl = _c_out(xl, m1l, ol, c_wo, i)
            if ctx_out:
                oc = _flash(qc, [kc], [vc])
                xc = _c_out(xc, m1c, oc, c_wo, i)
        xl = _ffn(xl, m2l, wgu, wd, l, 2 - 1)
        if ctx_out:
            xc = _ffn(xc, m2c, wgu, wd, l, 1)
    return xl
```

```python
import functools

import jax
import jax.numpy as jnp
from jax import lax
from jax.experimental import pallas as pl
from jax.experimental.pallas import tpu as pltpu

F32 = jnp.float32
BF16 = jnp.bfloat16

D_MODEL = 1024
DEPTH = 4
GRID_W = 64
D_FF = 2816
N_SUB = 3
N_MOD = 3 * N_SUB
FFN_HALF = 0.5
NORM_EPS = 1e-6
ROPE_THETA = 10000.0
DEEP_ALPHA = (2 * DEPTH) ** 0.25

A_HEADS, A_KV, A_HD = 8, 2, 64
WINDOW = 128
BLOCK = 128
B_HEADS, B_DK, B_DV = 4, 128, 128
CONV_K = 5
CHUNK = 64
C_HEADS, C_KV, C_HD = 8, 2, 128

A_Q = A_HEADS * A_HD
A_KVW = A_KV * A_HD
B_QK = B_HEADS * B_DK
B_VW = B_HEADS * B_DV
B_QKV = 2 * B_QK + B_VW
AB_IN = A_Q + 2 * A_KVW + B_QKV + B_VW + 4 * B_HEADS
N_GATE = 4 * B_HEADS
LANE = 128
AB_IN_PAD = AB_IN - N_GATE + LANE
OFF_AK = A_Q
OFF_AV = A_Q + A_KVW
OFF_BQKV = A_Q + 2 * A_KVW
OFF_BZ = OFF_BQKV + B_QKV
OFF_GATE = OFF_BZ + B_VW
C_QW = C_HEADS * C_HD
C_KW = C_KV * C_HD
C_IN = C_QW + 2 * C_KW
HALO = 8
NEG = -1e30
MIB = 1024 * 1024


def _cparams(sem, vmem_mib):
    return pltpu.CompilerParams(dimension_semantics=sem, vmem_limit_bytes=vmem_mib * MIB)


def _dot(a, b):
    return jnp.dot(a, b, preferred_element_type=F32)


def _dot_nt(a, b):
    return lax.dot_general(a, b, (((1,), (1,)), ((), ())), preferred_element_type=F32)


def _silu(x):
    return x * jax.nn.sigmoid(x)


def _res_ln(x, y, gate, g, b):
    z = DEEP_ALPHA * x + gate * y
    mu = jnp.mean(z, axis=-1, keepdims=True)
    zc = z - mu
    var = jnp.mean(zc * zc, axis=-1, keepdims=True)
    return zc * lax.rsqrt(var + NORM_EPS) * g + b


def _token_tile(t):
    return 512 if t % 512 == 0 else 256


def _mod_body(c_ref, w_ref, b_ref, o_ref):
    c = c_ref[...]
    o_ref[...] = jnp.dot(_silu(c), w_ref[...], preferred_element_type=F32,
                         precision=lax.Precision.HIGHEST) + b_ref[...]


def _modulation(cvec, ada_w, ada_b):
    depth, d, n = ada_w.shape
    rows = cvec.shape[0]
    tn = 2304
    return pl.pallas_call(
        _mod_body,
        grid=(depth, n // tn),
        in_specs=[pl.BlockSpec((rows, d), lambda l, j: (0, 0)),
                  pl.BlockSpec((None, d, tn), lambda l, j: (l, 0, j)),
                  pl.BlockSpec((None, 1, tn), lambda l, j: (l, 0, j))],
        out_specs=pl.BlockSpec((None, rows, tn), lambda l, j: (l, 0, j)),
        out_shape=jax.ShapeDtypeStruct((depth, rows, n), F32),
        compiler_params=_cparams(("parallel", "parallel"), 40),
        name="mod",
    )(cvec, ada_w, ada_b.reshape(depth, 1, n))


FFN_CHUNK = 256


def _ffn_body(x_ref, mod_ref, wgu_ref, wd_ref, o_ref):
    x = x_ref[...]
    mod = mod_ref[...]
    h = (x * (1.0 + mod[1:2]) + mod[0:1]).astype(BF16)
    fc = FFN_CHUNK if x.shape[0] >= 512 else D_FF // 2
    y = None
    for j in range(D_FF // fc):
        gt = _dot(h, wgu_ref[:, j * fc:(j + 1) * fc])
        up = _dot(h, wgu_ref[:, D_FF + j * fc:D_FF + (j + 1) * fc])
        act = (_silu(gt) * up).astype(BF16)
        part = _dot(act, wd_ref[j * fc:(j + 1) * fc, :])
        y = part if y is None else y + part
    o_ref[...] = _res_ln(x, y, FFN_HALF * mod[2:3], mod[3:4], mod[4:5])


def _ffn(x, mod, wgu, wd, l, s):
    bsz, t, d = x.shape
    tm = _token_tile(t)
    per_batch = mod.shape[0] > 1
    return pl.pallas_call(
        _ffn_body,
        grid=(bsz, t // tm),
        in_specs=[pl.BlockSpec((None, tm, d), lambda b, i: (b, i, 0)),
                  pl.BlockSpec((None, 8, d), lambda b, i: (b if per_batch else 0, 0, 0)),
                  pl.BlockSpec((None, None, d, 2 * D_FF), lambda b, i: (l, s, 0, 0),
                               pipeline_mode=pl.Buffered(1)),
                  pl.BlockSpec((None, None, D_FF, d), lambda b, i: (l, s, 0, 0),
                               pipeline_mode=pl.Buffered(1))],
        out_specs=pl.BlockSpec((None, tm, d), lambda b, i: (b, i, 0)),
        out_shape=jax.ShapeDtypeStruct(x.shape, F32),
        compiler_params=_cparams(("parallel", "parallel"), 52),
        name="ffn",
    )(x, mod, wgu, wd)


def _rope_full(x, cos2, sin2):
    return x * cos2 + pltpu.roll(x, C_HD // 2, 1) * sin2


def _c_prep_body(*refs, rope):
    if rope:
        x_ref, mod_ref, w_ref, qn_ref, kn_ref, cos_ref, sin_ref, q_ref, k_ref, v_ref = refs
        cos2, sin2 = cos_ref[...], sin_ref[...]
    else:
        x_ref, mod_ref, w_ref, qn_ref, kn_ref, q_ref, k_ref, v_ref = refs
    x = x_ref[...]
    mod = mod_ref[...]
    h = (x * (1.0 + mod[1:2]) + mod[0:1]).astype(BF16)
    qkv = _dot(h, w_ref[...])
    qn = qn_ref[...] * (C_HD ** -0.5 * LOG2E)
    kn = kn_ref[...]

    heads = [(qkv[:, j * C_HD:(j + 1) * C_HD], qn) for j in range(C_HEADS)]
    heads += [(qkv[:, C_QW + j * C_HD:C_QW + (j + 1) * C_HD], kn) for j in range(C_KV)]
    ms = [jnp.mean(a * a, axis=-1, keepdims=True) for a, _ in heads]
    outs = [a * lax.rsqrt(m + NORM_EPS) * gain for (a, gain), m in zip(heads, ms)]
    if rope:
        outs = [_rope_full(a, cos2, sin2) for a in outs]
    for j in range(C_HEADS):
        q_ref[:, j * C_HD:(j + 1) * C_HD] = outs[j].astype(BF16)
    for j in range(C_KV):
        k_ref[:, j * C_HD:(j + 1) * C_HD] = outs[C_HEADS + j].astype(BF16)
    v_ref[...] = qkv[:, C_QW + C_KW:].astype(BF16)


def _c_prep(x, mod, w_in, l, qn, kn, tables):
    bsz, t, d = x.shape
    tm = _token_tile(t)
    per_batch = mod.shape[0] > 1
    rope = tables is not None
    in_specs = [pl.BlockSpec((None, tm, d), lambda b, i: (b, i, 0)),
                pl.BlockSpec((None, 8, d), lambda b, i: (b if per_batch else 0, 0, 0)),
                pl.BlockSpec((None, d, C_IN), lambda b, i: (l, 0, 0), pipeline_mode=pl.Buffered(1)),
                pl.BlockSpec((1, C_HD), lambda b, i: (0, 0)),
                pl.BlockSpec((1, C_HD), lambda b, i: (0, 0))]
    args = [x, mod, w_in, qn, kn]
    if rope:
        in_specs += [pl.BlockSpec((tm, C_HD), lambda b, i: (i, 0))] * 2
        args += list(tables)
    return pl.pallas_call(
        functools.partial(_c_prep_body, rope=rope),
        grid=(bsz, t // tm),
        in_specs=in_specs,
        out_specs=[pl.BlockSpec((None, tm, C_QW), lambda b, i: (b, i, 0)),
                   pl.BlockSpec((None, tm, C_KW), lambda b, i: (b, i, 0)),
                   pl.BlockSpec((None, tm, C_KW), lambda b, i: (b, i, 0))],
        out_shape=[jax.ShapeDtypeStruct((bsz, t, C_QW), BF16),
                   jax.ShapeDtypeStruct((bsz, t, C_KW), BF16),
                   jax.ShapeDtypeStruct((bsz, t, C_KW), BF16)],
        compiler_params=_cparams(("parallel", "parallel"), 40),
        name="c_prep",
    )(*args)


C_GROUP = C_HEADS // C_KV
TQ_C = 128
KC_C = 256
RB_C = 64
FLASH_UNROLL = 4
LOG2E = 1.4426950408889634


def _flash_body(q_ref, k_ref, v_ref, o_ref, q_scr, s_scr, p0_scr, p1_scr, m_scr, mb_scr, acc_scr, *,
                nch, tail_keys):
    i = pl.program_id(2)
    slot_a = i % 2
    slot_b = 1 - slot_a
    rows = q_scr.shape[0]
    tq = q_ref.shape[0]

    @pl.when(i == 0)
    def _():
        s_scr[1] = jnp.zeros(s_scr.shape[1:], F32)
        m_scr[1] = jnp.zeros(m_scr.shape[1:], F32)

    for g in range(C_GROUP):
        q_scr[g * tq:(g + 1) * tq, :] = q_ref[:, g * C_HD:(g + 1) * C_HD]
    m_cur = jnp.max(m_scr[slot_b], axis=-1, keepdims=True)
    mb_scr[...] = jnp.broadcast_to(m_cur, (rows, LANE))
    m_scr[slot_a] = jnp.full((rows, LANE), NEG, F32)
    acc_scr[...] = jnp.zeros((rows, 2 * C_HD), F32)

    def probs(j, p_scr):
        for r0 in range(0, rows, RB_C):
            mb = mb_scr[r0:r0 + RB_C, :]
            for c0 in range(0, KC_C, LANE):
                p = jnp.exp2(s_scr[slot_b, j, r0:r0 + RB_C, c0:c0 + LANE] - mb)
                p_scr[r0:r0 + RB_C, c0:c0 + LANE] = p.astype(BF16)

    def pv(j, p_scr):
        k0 = pl.multiple_of(j * KC_C, KC_C)
        acc_scr[...] += _dot(p_scr[...], v_ref[pl.ds(k0, KC_C), :])

    def logits(j, n_keys=KC_C):
        k0 = pl.multiple_of(j * KC_C, KC_C)
        s = _dot_nt(q_scr[...], k_ref[pl.ds(k0, KC_C), :])
        if n_keys < KC_C:
            s = jnp.where(lax.broadcasted_iota(jnp.int32, (1, KC_C), 1) < n_keys, s, NEG)
        s_scr[slot_a, j] = s
        mt = s[:, 0:LANE]
        for c0 in range(LANE, KC_C, LANE):
            mt = jnp.maximum(mt, s[:, c0:c0 + LANE])
        m_scr[slot_a] = jnp.maximum(m_scr[slot_a], mt)

    probs(0, p0_scr)

    def pair(t, carry):
        j = 2 * t
        logits(j)
        pv(j, p0_scr)
        probs(j + 1, p1_scr)
        logits(j + 1)
        pv(j + 1, p1_scr)
        probs(j + 2, p0_scr)
        return carry

    lax.fori_loop(0, (nch - 1) // 2, pair, 0, unroll=FLASH_UNROLL)
    pv(nch - 1, p0_scr)
    logits(nch - 1, tail_keys)

    acc = acc_scr[...]
    o = acc[:, :C_HD] / acc[:, C_HD:C_HD + 1]
    for g in range(C_GROUP):
        o_ref[:, g * C_HD:(g + 1) * C_HD] = o[g * tq:(g + 1) * tq].astype(BF16)


def _flash(q, ks, vs):
    bsz, t, _ = q.shape
    n_keys = sum(a.shape[1] for a in ks)
    nch = -(-n_keys // KC_C)
    nch += 1 - nch % 2
    n = nch * KC_C
    tail_keys = n_keys - (nch - 1) * KC_C
    assert 0 < tail_keys <= KC_C
    pad = [jnp.zeros((bsz, n - n_keys, C_KW), BF16)] if n > n_keys else []
    k = jnp.concatenate(list(ks) + pad, axis=1)
    ones = jnp.broadcast_to((jnp.arange(C_HD) == 0).astype(BF16), (bsz, n, C_HD))
    v = jnp.concatenate(
        [jnp.concatenate([a[:, :, j * C_HD:(j + 1) * C_HD] for a in list(vs) + pad], axis=1) if i == 0 else ones
         for j in range(C_KV) for i in range(2)], axis=2)
    tq = TQ_C
    nq = t // tq
    rows = C_GROUP * tq
    gw = C_GROUP * C_HD
    return pl.pallas_call(
        functools.partial(_flash_body, nch=nch, tail_keys=tail_keys),
        grid=(bsz, C_KV, nq + 1),
        in_specs=[pl.BlockSpec((None, tq, gw), lambda b, h, i: (b, jnp.minimum(i, nq - 1), h)),
                  pl.BlockSpec((None, n, C_HD), lambda b, h, i: (b, 0, h)),
                  pl.BlockSpec((None, n, 2 * C_HD), lambda b, h, i: (b, 0, h))],
        out_specs=pl.BlockSpec((None, tq, gw), lambda b, h, i: (b, jnp.maximum(i - 1, 0), h)),
        out_shape=jax.ShapeDtypeStruct((bsz, t, C_QW), BF16),
        scratch_shapes=[pltpu.VMEM((rows, C_HD), BF16),
                        pltpu.VMEM((2, nch, rows, KC_C), F32),
                        pltpu.VMEM((rows, KC_C), BF16),
                        pltpu.VMEM((rows, KC_C), BF16),
                        pltpu.VMEM((2, rows, LANE), F32),
                        pltpu.VMEM((rows, LANE), F32),
                        pltpu.VMEM((rows, 2 * C_HD), F32)],
        compiler_params=_cparams(("parallel", "parallel", "arbitrary"), 48),
        name="flash",
    )(q, k, v)


def _c_out_body(x_ref, mod_ref, o_ref, w_ref, out_ref):
    mod = mod_ref[...]
    y = _dot(o_ref[...], w_ref[...])
    out_ref[...] = _res_ln(x_ref[...], y, mod[2:3], mod[3:4], mod[4:5])


def _c_out(x, mod, o, w_out, l):
    bsz, t, d = x.shape
    tm = _token_tile(t)
    per_batch = mod.shape[0] > 1
    return pl.pallas_call(
        _c_out_body,
        grid=(bsz, t // tm),
        in_specs=[pl.BlockSpec((None, tm, d), lambda b, i: (b, i, 0)),
                  pl.BlockSpec((None, 8, d), lambda b, i: (b if per_batch else 0, 0, 0)),
                  pl.BlockSpec((None, tm, C_QW), lambda b, i: (b, i, 0)),
                  pl.BlockSpec((None, C_QW, d), lambda b, i: (l, 0, 0), pipeline_mode=pl.Buffered(1))],
        out_specs=pl.BlockSpec((None, tm, d), lambda b, i: (b, i, 0)),
        out_shape=jax.ShapeDtypeStruct(x.shape, F32),
        compiler_params=_cparams(("parallel", "parallel"), 32),
        name="c_out",
    )(x, mod, o, w_out)


def _rope_half_tile(x, cos2, sin2, lane_lo):
    half = A_HD // 2
    partner = jnp.where(lane_lo, pltpu.roll(x, LANE - half, 1), pltpu.roll(x, half, 1))
    return x * cos2 + partner * sin2


def _gate_act(raw, is_beta, a_log, dt_bias):
    sp_in = raw + dt_bias
    softplus = jnp.maximum(sp_in, 0.0) + jnp.log(1.0 + jnp.exp(-jnp.abs(sp_in)))
    return jnp.where(is_beta, jax.nn.sigmoid(raw), -jnp.exp(a_log) * softplus)


def _ab_prep_body(*refs, rope, tm):
    if rope:
        (xp_ref, x_ref, xn_ref, mod_ref, w_ref, cw_ref, gp_ref, cos_ref, sin_ref,
         aq_ref, ak_ref, av_ref, bq_ref, bk_ref, bv_ref, z_ref, gc_ref, gr_ref, cbuf) = refs
    else:
        (xp_ref, x_ref, xn_ref, mod_ref, w_ref, cw_ref, gp_ref,
         aq_ref, ak_ref, av_ref, bq_ref, bk_ref, bv_ref, z_ref, gc_ref, gr_ref, cbuf) = refs
    i = pl.program_id(1)
    nt = pl.num_programs(1)
    mod = mod_ref[...]
    scale1, shift = 1.0 + mod[1:2], mod[0:1]
    h = (x_ref[...] * scale1 + shift).astype(BF16)
    halo = jnp.concatenate([xp_ref[...], xn_ref[...]], axis=0)
    h_all = jnp.concatenate([h, (halo * scale1 + shift).astype(BF16)], axis=0)
    proj_all = _dot(h_all, w_ref[...])
    proj = proj_all[:tm]

    if rope:
        cos2, sin2 = cos_ref[...], sin_ref[...]
        lane_lo = (lax.broadcasted_iota(jnp.int32, (tm, LANE), 1) % A_HD) < (A_HD // 2)
    for j in range(A_Q // LANE):
        a = proj[:, j * LANE:(j + 1) * LANE]
        if rope:
            a = _rope_half_tile(a, cos2, sin2, lane_lo)
        aq_ref[:, j * LANE:(j + 1) * LANE] = (a * (A_HD ** -0.5 * LOG2E)).astype(BF16)
    a = proj[:, OFF_AK:OFF_AK + LANE]
    if rope:
        a = _rope_half_tile(a, cos2, sin2, lane_lo)
    ak_ref[...] = a.astype(BF16)
    av_ref[...] = proj[:, OFF_AV:OFF_AV + LANE].astype(BF16)
    z_ref[...] = proj[:, OFF_BZ:OFF_BZ + B_VW]

    gp = gp_ref[...]
    lane = lax.broadcasted_iota(jnp.int32, (1, LANE), 1)
    gates = _gate_act(proj[:, OFF_GATE:OFF_GATE + LANE], lane < 2 * B_HEADS, gp[0:1], gp[1:2])
    gc_ref[...] = gates
    gr_ref[...] = gates.T[:N_GATE, :]

    cbuf[0:HALO, :] = proj_all[tm:tm + HALO, OFF_BQKV:OFF_BQKV + B_QKV] * jnp.where(i > 0, 1.0, 0.0)
    cbuf[HALO:HALO + tm, :] = proj[:, OFF_BQKV:OFF_BQKV + B_QKV]
    cbuf[HALO + tm:, :] = proj_all[tm + HALO:, OFF_BQKV:OFF_BQKV + B_QKV] * jnp.where(i < nt - 1, 1.0, 0.0)
    cw = cw_ref[...]
    pad = CONV_K // 2
    for part, dst in enumerate((bq_ref, bk_ref, bv_ref)):
        for hd in range(B_HEADS):
            c0 = part * B_QK + hd * B_DK
            acc = None
            for j in range(CONV_K):
                term = cbuf[HALO - pad + j:HALO - pad + j + tm, c0:c0 + B_DK] * cw[j:j + 1, c0:c0 + B_DK]
                acc = term if acc is None else acc + term
            acc = _silu(acc)
            if part < 2:
                acc = acc * lax.rsqrt(jnp.sum(acc * acc, axis=-1, keepdims=True) + NORM_EPS)
            if part == 0:
                acc = acc * (B_DK ** -0.5)
            dst[:, hd * B_DK:(hd + 1) * B_DK] = acc


def _ab_prep(x, mod, w_pad, conv_w, gate_p, l, tables):
    bsz, t, d = x.shape
    tm = _token_tile(t)
    nh = tm // HALO
    nblk = t // HALO
    per_batch = mod.shape[0] > 1
    rope = tables is not None
    const = dict(pipeline_mode=pl.Buffered(1))
    in_specs = [pl.BlockSpec((None, HALO, d), lambda b, i: (b, jnp.maximum(i * nh - 1, 0), 0)),
                pl.BlockSpec((None, tm, d), lambda b, i: (b, i, 0)),
                pl.BlockSpec((None, HALO, d), lambda b, i: (b, jnp.minimum((i + 1) * nh, nblk - 1), 0)),
                pl.BlockSpec((None, 8, d), lambda b, i: (b if per_batch else 0, 0, 0)),
                pl.BlockSpec((None, d, AB_IN_PAD), lambda b, i: (l, 0, 0), **const),
                pl.BlockSpec((None, 8, B_QKV), lambda b, i: (l, 0, 0)),
                pl.BlockSpec((None, 8, LANE), lambda b, i: (l, 0, 0))]
    args = [x, x, x, mod, w_pad, conv_w, gate_p]
    if rope:
        in_specs += [pl.BlockSpec((tm, LANE), lambda b, i: (i, 0))] * 2
        args += list(tables)
    tok = lambda w: pl.BlockSpec((None, tm, w), lambda b, i: (b, i, 0))
    shp = lambda w, dt: jax.ShapeDtypeStruct((bsz, t, w), dt)
    return pl.pallas_call(
        functools.partial(_ab_prep_body, rope=rope, tm=tm),
        grid=(bsz, t // tm),
        in_specs=in_specs,
        out_specs=[tok(A_Q), tok(A_KVW), tok(A_KVW), tok(B_QK), tok(B_QK), tok(B_VW), tok(B_VW), tok(LANE),
                   pl.BlockSpec((None, N_GATE, tm), lambda b, i: (b, 0, i))],
        out_shape=[shp(A_Q, BF16), shp(A_KVW, BF16), shp(A_KVW, BF16), shp(B_QK, F32), shp(B_QK, F32),
                   shp(B_VW, F32), shp(B_VW, F32), shp(LANE, F32),
                   jax.ShapeDtypeStruct((bsz, N_GATE, t), F32)],
        scratch_shapes=[pltpu.VMEM((tm + 2 * HALO, B_QKV), F32)],
        compiler_params=_cparams(("parallel", "parallel"), 52),
        name="ab_prep",
    )(*args)


A_GROUP = A_HEADS // A_KV


WIN_TILE = 1024
WIN_GROUP = 1
A_HEAD_PERM = tuple(h for t in range(A_GROUP) for h in (t, A_GROUP + t))


def _win_body(*refs, local, nqb):
    if local:
        q_ref, kp_ref, kc_ref, kn_ref, vp_ref, vc_ref, vn_ref, kx_ref, vx_ref, sink_ref, o_ref = refs
    else:
        q_ref, kx_ref, vx_ref, sink_ref, o_ref = refs
    lane = lax.broadcasted_iota(jnp.int32, (1, LANE), 1)
    kv_lanes = (lane < A_HD, lane >= A_HD)
    kx = kx_ref[...]
    vx = vx_ref[...]
    sink = sink_ref[...] * LOG2E
    if local:
        i = pl.program_id(1)
        nsteps = pl.num_programs(1)
        kcat = jnp.concatenate([kp_ref[...], kc_ref[...], kn_ref[...]], axis=0)
        vcat = jnp.concatenate([vp_ref[...], vc_ref[...], vn_ref[...]], axis=0)
        r = lax.broadcasted_iota(jnp.int32, (BLOCK, 3 * BLOCK), 0)
        c = lax.broadcasted_iota(jnp.int32, (BLOCK, 3 * BLOCK), 1)
        band = (c >= r) & (c <= r + 2 * WINDOW)
    zero = jnp.zeros((), BF16)
    sks = [jnp.concatenate([jnp.broadcast_to(sink[kv * A_GROUP + g:kv * A_GROUP + g + 1, 0:1], (BLOCK, 1))
                            for g in range(A_GROUP)], axis=0) for kv in range(A_KV)]
    def logits(qb0):
        units = []
        for qb in range(qb0, min(qb0 + WIN_GROUP, nqb)):
            r0 = qb * BLOCK
            tiles = [q_ref[r0:r0 + BLOCK, t * LANE:(t + 1) * LANE] for t in range(A_GROUP)]
            for kv in range(A_KV):
                qs = jnp.concatenate([jnp.where(kv_lanes[kv], tl, zero) for tl in tiles], axis=0)
                units.append(dict(qb=qb, kv=kv, r0=r0, qs=qs))
        for un in units:
            un["s_x"] = _dot_nt(un["qs"], kx)
            if local:
                un["s_l"] = _dot_nt(un["qs"], kcat[un["r0"]:un["r0"] + 3 * BLOCK])
        return units

    def finish(units):
        for un in units:
            m = jnp.maximum(jnp.max(un["s_x"], axis=-1, keepdims=True), sks[un["kv"]])
            if local:
                valid = band
                if un["qb"] == 0:
                    valid = valid & ((c >= BLOCK) | (i > 0))
                if un["qb"] == nqb - 1:
                    valid = valid & ((c < 2 * BLOCK) | (i < nsteps - 1))
                un["s_l"] = jnp.concatenate(
                    [jnp.where(valid, un["s_l"][g * BLOCK:(g + 1) * BLOCK], NEG) for g in range(A_GROUP)], axis=0)
                m = jnp.maximum(m, jnp.max(un["s_l"], axis=-1, keepdims=True))
            un["m"] = m
        for un in units:
            p_x = jnp.exp2(un["s_x"] - un["m"])
            un["den"] = jnp.sum(p_x, axis=-1, keepdims=True) + jnp.exp2(sks[un["kv"]] - un["m"])
            un["p_x"] = p_x.astype(BF16)
            if local:
                p_l = jnp.exp2(un["s_l"] - un["m"])
                un["den"] = un["den"] + jnp.sum(p_l, axis=-1, keepdims=True)
                un["p_l"] = p_l.astype(BF16)
        for un in units:
            o = _dot(un["p_x"], vx)
            if local:
                o = o + _dot(un["p_l"], vcat[un["r0"]:un["r0"] + 3 * BLOCK])
            un["o"] = o / un["den"]
        for u0 in range(0, len(units), A_KV):
            r0 = units[u0]["r0"]
            for t in range(A_GROUP):
                o_ref[r0:r0 + BLOCK, t * LANE:(t + 1) * LANE] = jnp.where(
                    kv_lanes[0], units[u0]["o"][t * BLOCK:(t + 1) * BLOCK],
                    units[u0 + 1]["o"][t * BLOCK:(t + 1) * BLOCK]).astype(BF16)

    starts = list(range(0, nqb, WIN_GROUP))
    pending = logits(starts[0])
    for nxt in starts[1:] + [None]:
        ahead = logits(nxt) if nxt is not None else None
        finish(pending)
        pending = ahead


def _win_attn(q, kx, vx, sink, l, k=None, v=None):
    bsz, t, _ = q.shape
    lc = kx.shape[1]
    local = k is not None
    tq = min(WIN_TILE, t)
    nqb = tq // BLOCK
    nb = t // BLOCK
    edge = lambda f: pl.BlockSpec((None, BLOCK, A_KVW), f)
    in_specs = [pl.BlockSpec((None, tq, A_Q), lambda b, i: (b, i, 0))]
    args = [q]
    if local:
        prv = lambda b, i: (b, jnp.maximum(i * nqb - 1, 0), 0)
        nxt = lambda b, i: (b, jnp.minimum((i + 1) * nqb, nb - 1), 0)
        cur = pl.BlockSpec((None, tq, A_KVW), lambda b, i: (b, i, 0))
        in_specs += [edge(prv), cur, edge(nxt), edge(prv), cur, edge(nxt)]
        args += [k, k, k, v, v, v]
    in_specs += [pl.BlockSpec((None, lc, A_KVW), lambda b, i: (b, 0, 0)),
                 pl.BlockSpec((None, lc, A_KVW), lambda b, i: (b, 0, 0)),
                 pl.BlockSpec((None, A_HEADS, LANE), lambda b, i: (l, 0, 0))]
    args += [kx, vx, sink]
    return pl.pallas_call(
        functools.partial(_win_body, local=local, nqb=nqb),
        grid=(bsz, t // tq),
        in_specs=in_specs,
        out_specs=pl.BlockSpec((None, tq, A_Q), lambda b, i: (b, i, 0)),
        out_shape=jax.ShapeDtypeStruct((bsz, t, A_Q), BF16),
        compiler_params=_cparams(("parallel", "parallel"), 40),
        name="win_attn",
    )(*args)


DELTA_BLOCK = 4 * CHUNK
DELTA_CHUNKS = DELTA_BLOCK // CHUNK
N_CHAIN = 2 * B_HEADS


def _split_bf16(a):
    hi = a.astype(BF16)
    return hi, (a - hi.astype(F32)).astype(BF16)


def _delta_intra(units, eye):
    for un in units:
        decay = jnp.exp(jnp.where(un["incl"], un["gcc"] - un["gcr"], NEG))
        kb = un["k"] * un["beta"]
        a = _dot_nt(jnp.concatenate([kb, un["q"]], axis=0).astype(BF16), un["k"].astype(BF16))
        un["p"] = jnp.where(un["strict"], a[:CHUNK] * decay, 0.0) * -1.0
        un["attn"] = (a[CHUNK:] * decay).astype(BF16)
        un["t"] = eye + un["p"]
        eg = jnp.exp(un["gcc"])
        un["rhs"] = jnp.concatenate([un["v"] * un["beta"], kb * eg], axis=1).astype(BF16)
        un["qd"] = un["q"] * eg
        un["kt_t"] = (un["k"] * jnp.exp(un["gtot"] - un["gcc"])).T.astype(BF16)
        un["dec"] = jnp.exp(un["gtot"])
    for un in units:
        pb = un["p"].astype(BF16)
        un["pb"] = _dot(pb, pb).astype(BF16)
    for _ in range(CHUNK.bit_length() - 3):
        for un in units:
            both = _dot(jnp.concatenate([un["pb"], un["t"].astype(BF16)], axis=0), un["pb"])
            un["t"] = un["t"] + both[CHUNK:]
            un["pb"] = both[:CHUNK].astype(BF16)
    for un in units:
        un["t"] = un["t"] + _dot(un["t"].astype(BF16), un["pb"])
    for un in units:
        uw = _dot(un["t"].astype(BF16), un["rhs"])
        un["u"] = uw[:, :B_DV]
        un["wq"] = jnp.concatenate([uw[:, B_DV:], un["qd"]], axis=0).astype(BF16)


def _delta_body(qf_ref, kf_ref, vf_ref, gcf_ref, grf_ref, qb_ref, kb_ref, vb_ref, gcb_ref, grb_ref, s0_ref,
                of_ref, ob_ref, s_ref):
    @pl.when(pl.program_id(1) == 0)
    def _():
        s_ref[...] = s0_ref[...]

    li = lax.broadcasted_iota(jnp.int32, (CHUNK, CHUNK), 0)
    lj = lax.broadcasted_iota(jnp.int32, (CHUNK, CHUNK), 1)
    eye = jnp.where(li == lj, 1.0, 0.0)
    tri_l = jnp.where(li >= lj, 1.0, 0.0).astype(BF16)
    tri_u = jnp.where(li <= lj, 1.0, 0.0).astype(BF16)

    dirs = ((qf_ref, kf_ref, vf_ref, gcf_ref, grf_ref, of_ref, tuple(range(DELTA_CHUNKS)), tri_l, tri_u, li >= lj, li > lj, CHUNK - 1),
            (qb_ref, kb_ref, vb_ref, gcb_ref, grb_ref, ob_ref, tuple(reversed(range(DELTA_CHUNKS))), tri_u, tri_l, li <= lj, li < lj, 0))
    units = {}
    for d, (q_ref, k_ref, v_ref, gc_ref, gr_ref, o_ref, order, tri_c, tri_r, incl, strict, last) in enumerate(dirs):
        for c in order:
            r0 = c * CHUNK
            gcol = gc_ref[r0:r0 + CHUNK, :]
            ghi, glo = _split_bf16(gcol)
            cum_c = _dot(tri_c, ghi) + _dot(tri_c, glo)
            grow = gr_ref[:, r0:r0 + CHUNK]
            rhi, rlo = _split_bf16(grow)
            cum_r = _dot(rhi, tri_r) + _dot(rlo, tri_r)
            for hd in range(B_HEADS):
                col = d * B_HEADS + hd
                gl = 2 * B_HEADS + col
                lo, hi = hd * B_DK, (hd + 1) * B_DK
                units[(d, c, hd)] = dict(
                    q=q_ref[r0:r0 + CHUNK, lo:hi], k=k_ref[r0:r0 + CHUNK, lo:hi], v=v_ref[r0:r0 + CHUNK, lo:hi],
                    beta=gcol[:, col:col + 1], gcc=cum_c[:, gl:gl + 1], gcr=cum_r[gl:gl + 1, :],
                    gtot=cum_c[last:last + 1, gl:gl + 1], incl=incl, strict=strict)
    _delta_intra(list(units.values()), eye)

    chains = [(d, hd) for d in range(2) for hd in range(B_HEADS)]
    state = [s_ref[d * B_HEADS + hd] for d, hd in chains]
    for step in range(DELTA_CHUNKS):
        cur = [units[(d, dirs[d][6][step], hd)] for d, hd in chains]
        ws = [_dot(un["wq"], s.astype(BF16)) for un, s in zip(cur, state)]
        v_new = [(un["u"] - w[:CHUNK]).astype(BF16) for un, w in zip(cur, ws)]
        outs = [w[CHUNK:] + _dot(un["attn"], vn) for un, w, vn in zip(cur, ws, v_new)]
        state = [s * un["dec"] + _dot(un["kt_t"], vn) for un, s, vn in zip(cur, state, v_new)]
        for (d, hd), o in zip(chains, outs):
            r0 = dirs[d][6][step] * CHUNK
            dirs[d][5][r0:r0 + CHUNK, hd * B_DV:(hd + 1) * B_DV] = o
    for (d, hd), s in zip(chains, state):
        s_ref[d * B_HEADS + hd] = s


def _delta(bq, bk, bv, gcol, grow, s0):
    bsz, t, _ = bq.shape
    ns = t // DELTA_BLOCK
    fwd = lambda w: pl.BlockSpec((None, DELTA_BLOCK, w), lambda b, s: (b, s, 0))
    bwd = lambda w: pl.BlockSpec((None, DELTA_BLOCK, w), lambda b, s: (b, ns - 1 - s, 0))
    st = pl.BlockSpec((None, N_CHAIN, B_DK, B_DV), lambda b, s: (b, 0, 0, 0))
    return pl.pallas_call(
        _delta_body,
        grid=(bsz, ns),
        in_specs=[fwd(B_QK), fwd(B_QK), fwd(B_VW), fwd(LANE),
                  pl.BlockSpec((None, N_GATE, DELTA_BLOCK), lambda b, s: (b, 0, s)),
                  bwd(B_QK), bwd(B_QK), bwd(B_VW), bwd(LANE),
                  pl.BlockSpec((None, N_GATE, DELTA_BLOCK), lambda b, s: (b, 0, ns - 1 - s)),
                  st],
        out_specs=[fwd(B_VW), bwd(B_VW), st],
        out_shape=[jax.ShapeDtypeStruct((bsz, t, B_VW), F32), jax.ShapeDtypeStruct((bsz, t, B_VW), F32),
                   jax.ShapeDtypeStruct((bsz, N_CHAIN, B_DK, B_DV), F32)],
        compiler_params=_cparams(("parallel", "arbitrary"), 32),
        name="delta",
    )(bq, bk, bv, gcol, grow, bq, bk, bv, gcol, grow, s0)


def _ab_out_body(x_ref, mod_ref, oa_ref, of_ref, ob_ref, z_ref, gn_ref, w_ref, out_ref):
    mod = mod_ref[...]
    gn = gn_ref[...]
    y = _dot(oa_ref[...], w_ref[0:A_Q, :])
    for hd in range(B_HEADS):
        lo, hi = hd * B_DV, (hd + 1) * B_DV
        o = of_ref[:, lo:hi] + ob_ref[:, lo:hi]
        o = o * lax.rsqrt(jnp.mean(o * o, axis=-1, keepdims=True) + NORM_EPS) * gn
        o = (o * _silu(z_ref[:, lo:hi])).astype(BF16)
        y = y + _dot(o, w_ref[A_Q + lo:A_Q + hi, :])
    out_ref[...] = _res_ln(x_ref[...], y, mod[2:3], mod[3:4], mod[4:5])


def _ab_out(x, mod, oa, of, ob, z, gnorm, w_out, l):
    bsz, t, d = x.shape
    tm = _token_tile(t)
    per_batch = mod.shape[0] > 1
    tok = lambda w: pl.BlockSpec((None, tm, w), lambda b, i: (b, i, 0))
    return pl.pallas_call(
        _ab_out_body,
        grid=(bsz, t // tm),
        in_specs=[tok(d),
                  pl.BlockSpec((None, 8, d), lambda b, i: (b if per_batch else 0, 0, 0)),
                  tok(A_Q), tok(B_VW), tok(B_VW), tok(B_VW),
                  pl.BlockSpec((None, 1, B_DV), lambda b, i: (l, 0, 0)),
                  pl.BlockSpec((None, A_Q + B_VW, d), lambda b, i: (l, 0, 0), pipeline_mode=pl.Buffered(1))],
        out_specs=tok(d),
        out_shape=jax.ShapeDtypeStruct(x.shape, F32),
        compiler_params=_cparams(("parallel", "parallel"), 40),
        name="ab_out",
    )(x, mod, oa, of, ob, z, gnorm, w_out)


def _rope_tables(rows, head_dim):
    n_freq = head_dim // 4
    inv = ROPE_THETA ** (-jnp.arange(n_freq, dtype=F32) / n_freq)
    r, col = jnp.meshgrid(jnp.arange(rows, dtype=F32), jnp.arange(GRID_W, dtype=F32), indexing='ij')
    r, col = r.reshape(-1), col.reshape(-1)
    ang = jnp.concatenate([r[:, None] * inv, col[:, None] * inv], axis=-1)
    cos, sin = jnp.cos(ang), jnp.sin(ang)
    cos2 = jnp.concatenate([cos, cos], axis=-1)
    sin2 = jnp.concatenate([-sin, sin], axis=-1)
    rep = LANE // head_dim
    return jnp.tile(cos2, (1, rep)), jnp.tile(sin2, (1, rep))


def _pad_rows(a, rows):
    return jnp.pad(a, ((0, 0), (0, rows - a.shape[1]), (0, 0)))


def kernel(x, c, ctx, c_ctx, ada_w, ada_b, ln_g, ln_b, ffn_w_gu, ffn_w_down, ab_w_in, ab_conv_w, ab_a_log,
           ab_dt_bias, ab_gnorm, ab_sink, ab_w_out, c_w_in, c_q_norm, c_k_norm, c_w_out):
    bsz, t, d = x.shape
    depth = ada_w.shape[0]
    n_even = ab_w_in.shape[0]
    rows = t // GRID_W
    tab_a = _rope_tables(rows, A_HD)
    tab_c = _rope_tables(rows, C_HD)

    wgu = ffn_w_gu.astype(BF16)
    wd = ffn_w_down.astype(BF16)
    head_cols = jnp.asarray([h * A_HD + j for h in A_HEAD_PERM for j in range(A_HD)], jnp.int32)
    ab_w_in_p = jnp.concatenate([jnp.take(ab_w_in[:, :, :A_Q], head_cols, axis=2), ab_w_in[:, :, A_Q:]], axis=2)
    ab_w_pad = jnp.pad(ab_w_in_p, ((0, 0), (0, 0), (0, AB_IN_PAD - AB_IN))).astype(BF16)
    ab_wo = jnp.concatenate([jnp.take(ab_w_out[:, :A_Q], head_cols, axis=1), ab_w_out[:, A_Q:]], axis=1).astype(BF16)
    c_wi = c_w_in.astype(BF16)
    c_wo = c_w_out.astype(BF16)
    conv_w = _pad_rows(ab_conv_w, 8)
    zeros8 = jnp.zeros((n_even, 2 * B_HEADS), F32)
    a_log16 = jnp.concatenate([zeros8, ab_a_log.reshape(n_even, 2 * B_HEADS)], axis=1)
    dtb16 = jnp.concatenate([zeros8, ab_dt_bias.reshape(n_even, 2 * B_HEADS)], axis=1)
    gate_p = _pad_rows(jnp.pad(jnp.stack([a_log16, dtb16], axis=1), ((0, 0), (0, 0), (0, LANE - N_GATE))), 8)
    sink = jnp.broadcast_to(ab_sink[:, :, None], (n_even, A_HEADS, LANE))
    gnorm = ab_gnorm.reshape(n_even, 1, B_DV)
    qn = c_q_norm.reshape(-1, 1, C_HD)
    kn = c_k_norm.reshape(-1, 1, C_HD)

    nrow = ((bsz + 1 + 7) // 8) * 8
    cvec = jnp.pad(jnp.concatenate([c, c_ctx[None]], axis=0), ((0, nrow - bsz - 1), (0, 0)))
    m = _modulation(cvec, ada_w, ada_b).reshape(depth, nrow, N_MOD, d)

    def mod_rows(l, s):
        ln = jnp.stack([ln_g[l, s], ln_b[l, s]], axis=0)
        rows_l = jnp.concatenate([m[l, :bsz, 3 * s:3 * s + 3], jnp.broadcast_to(ln, (bsz, 2, d)),
                                  jnp.zeros((bsz, 3, d), F32)], axis=1)
        rows_c = jnp.concatenate([m[l, bsz:bsz + 1, 3 * s:3 * s + 3], ln[None],
                                  jnp.zeros((1, 3, d), F32)], axis=1)
        return rows_l, rows_c

    xl, xc = x, ctx
    for l in range(depth):
        ctx_out = l < depth - 1
        i = l // 2
        m0l, m0c = mod_rows(l, 0)
        m1l, m1c = mod_rows(l, 1)
        m2l, m2c = mod_rows(l, 2)
        xl = _ffn(xl, m0l, wgu, wd, l, 0)
        xc = _ffn(xc, m0c, wgu, wd, l, 0)
        if l % 2 == 0:
            pc = _ab_prep(xc, m1c, ab_w_pad, conv_w, gate_p, i, None)
            pl_ = _ab_prep(xl, m1l, ab_w_pad, conv_w, gate_p, i, tab_a)
            aqc, akc, avc, bqc, bkc, bvc, zc, gcc, grc = pc
            aql, akl, avl, bql, bkl, bvl, zl, gcl, grl = pl_
            ol_a = _win_attn(aql, akc, avc, sink, i, akl, avl)
            s0 = jnp.zeros((bsz, N_CHAIN, B_DK, B_DV), F32)
            oc_f, oc_b, s_ctx = _delta(bqc, bkc, bvc, gcc, grc, s0)
            ol_f, ol_b, _ = _delta(bql, bkl, bvl, gcl, grl, s_ctx)
            xl = _ab_out(xl, m1l, ol_a, ol_f, ol_b, zl, gnorm, ab_wo, i)
            if ctx_out:
                oc_a = _win_attn(aqc, akc, avc, sink, i)
                xc = _ab_out(xc, m1c, oc_a, oc_f, oc_b, zc, gnorm, ab_wo, i)
        else:
            qc, kc, vc = _c_prep(xc, m1c, c_wi, i, qn[i], kn[i], None)
            ql, kl, vl = _c_prep(xl, m1l, c_wi, i, qn[i], kn[i], tab_c)
            ol = _flash(ql, [kc, kl], [vc, vl])
            xl = _c_out(xl, m1l, ol, c_wo, i)
            if ctx_out:
                oc = _flash(qc, [kc], [vc])
                xc = _c_out(xc, m1c, oc, c_wo, i)
        xl = _ffn(xl, m2l, wgu, wd, l, 2 - 1)
        if ctx_out:
            xc = _ffn(xc, m2c, wgu, wd, l, 1)
    return xl
```

```python
import functools

import jax
import jax.numpy as jnp
from jax import lax
from jax.experimental import pallas as pl
from jax.experimental.pallas import tpu as pltpu

F32 = jnp.float32
BF16 = jnp.bfloat16

D_MODEL = 1024
DEPTH = 4
GRID_W = 64
D_FF = 2816
N_SUB = 3
N_MOD = 3 * N_SUB
FFN_HALF = 0.5
NORM_EPS = 1e-6
ROPE_THETA = 10000.0
DEEP_ALPHA = (2 * DEPTH) ** 0.25

A_HEADS, A_KV, A_HD = 8, 2, 64
WINDOW = 128
BLOCK = 128
B_HEADS, B_DK, B_DV = 4, 128, 128
CONV_K = 5
CHUNK = 64
C_HEADS, C_KV, C_HD = 8, 2, 128

A_Q = A_HEADS * A_HD
A_KVW = A_KV * A_HD
B_QK = B_HEADS * B_DK
B_VW = B_HEADS * B_DV
B_QKV = 2 * B_QK + B_VW
AB_IN = A_Q + 2 * A_KVW + B_QKV + B_VW + 4 * B_HEADS
N_GATE = 4 * B_HEADS
LANE = 128
AB_IN_PAD = AB_IN - N_GATE + LANE
OFF_AK = A_Q
OFF_AV = A_Q + A_KVW
OFF_BQKV = A_Q + 2 * A_KVW
OFF_BZ = OFF_BQKV + B_QKV
OFF_GATE = OFF_BZ + B_VW
C_QW = C_HEADS * C_HD
C_KW = C_KV * C_HD
C_IN = C_QW + 2 * C_KW
HALO = 8
NEG = -1e30
MIB = 1024 * 1024


def _cparams(sem, vmem_mib):
    return pltpu.CompilerParams(dimension_semantics=sem, vmem_limit_bytes=vmem_mib * MIB)


def _dot(a, b):
    return jnp.dot(a, b, preferred_element_type=F32)


def _dot_nt(a, b):
    return lax.dot_general(a, b, (((1,), (1,)), ((), ())), preferred_element_type=F32)


def _silu(x):
    return x * jax.nn.sigmoid(x)


def _res_ln(x, y, gate, g, b):
    z = DEEP_ALPHA * x + gate * y
    mu = jnp.mean(z, axis=-1, keepdims=True)
    zc = z - mu
    var = jnp.mean(zc * zc, axis=-1, keepdims=True)
    return zc * lax.rsqrt(var + NORM_EPS) * g + b


def _token_tile(t):
    return 512 if t % 512 == 0 else 256


def _mod_body(c_ref, w_ref, b_ref, o_ref):
    c = c_ref[...]
    o_ref[...] = jnp.dot(_silu(c), w_ref[...], preferred_element_type=F32,
                         precision=lax.Precision.HIGHEST) + b_ref[...]


def _modulation(cvec, ada_w, ada_b):
    depth, d, n = ada_w.shape
    rows = cvec.shape[0]
    tn = 2304
    return pl.pallas_call(
        _mod_body,
        grid=(depth, n // tn),
        in_specs=[pl.BlockSpec((rows, d), lambda l, j: (0, 0)),
                  pl.BlockSpec((None, d, tn), lambda l, j: (l, 0, j)),
                  pl.BlockSpec((None, 1, tn), lambda l, j: (l, 0, j))],
        out_specs=pl.BlockSpec((None, rows, tn), lambda l, j: (l, 0, j)),
        out_shape=jax.ShapeDtypeStruct((depth, rows, n), F32),
        compiler_params=_cparams(("parallel", "parallel"), 40),
        name="mod",
    )(cvec, ada_w, ada_b.reshape(depth, 1, n))


FFN_CHUNK = 256


def _ffn_body(x_ref, mod_ref, wgu_ref, wd_ref, o_ref):
    _ffn_tile(x_ref[...], mod_ref[...], wgu_ref, wd_ref, o_ref)


def _c_out_ffn_body(x_ref, mod1_ref, mod_ref, a_ref, wo_ref, wgu_ref, wd_ref, o_ref):
    m1 = mod1_ref[...]
    x1 = _res_ln(x_ref[...], _dot(a_ref[...], wo_ref[...]), m1[2:3], m1[3:4], m1[4:5])
    _ffn_tile(x1, mod_ref[...], wgu_ref, wd_ref, o_ref)


def _ffn_tile(x, mod, wgu_ref, wd_ref, o_ref):
    h = (x * (1.0 + mod[1:2]) + mod[0:1]).astype(BF16)
    fc = FFN_CHUNK if x.shape[0] >= 512 else D_FF // 2
    y = None
    for j in range(D_FF // fc):
        gt = _dot(h, wgu_ref[:, j * fc:(j + 1) * fc])
        up = _dot(h, wgu_ref[:, D_FF + j * fc:D_FF + (j + 1) * fc])
        act = (_silu(gt) * up).astype(BF16)
        part = _dot(act, wd_ref[j * fc:(j + 1) * fc, :])
        y = part if y is None else y + part
    o_ref[...] = _res_ln(x, y, FFN_HALF * mod[2:3], mod[3:4], mod[4:5])


def _ffn(x, mod, wgu, wd, l, s):
    bsz, t, d = x.shape
    tm = _token_tile(t)
    per_batch = mod.shape[0] > 1
    return pl.pallas_call(
        _ffn_body,
        grid=(bsz, t // tm),
        in_specs=[pl.BlockSpec((None, tm, d), lambda b, i: (b, i, 0)),
                  pl.BlockSpec((None, 8, d), lambda b, i: (b if per_batch else 0, 0, 0)),
                  pl.BlockSpec((None, None, d, 2 * D_FF), lambda b, i: (l, s, 0, 0),
                               pipeline_mode=pl.Buffered(1)),
                  pl.BlockSpec((None, None, D_FF, d), lambda b, i: (l, s, 0, 0),
                               pipeline_mode=pl.Buffered(1))],
        out_specs=pl.BlockSpec((None, tm, d), lambda b, i: (b, i, 0)),
        out_shape=jax.ShapeDtypeStruct(x.shape, F32),
        compiler_params=_cparams(("parallel", "parallel"), 52),
        name="ffn",
    )(x, mod, wgu, wd)


def _c_out_ffn(x, mod1, mod, a, w_out, li, wgu, wd, l, s):
    bsz, t, d = x.shape
    tm = _token_tile(t)
    per_batch = mod.shape[0] > 1
    tok = lambda w: pl.BlockSpec((None, tm, w), lambda b, i: (b, i, 0))
    rows = pl.BlockSpec((None, 8, d), lambda b, i: (b if per_batch else 0, 0, 0))
    const = dict(pipeline_mode=pl.Buffered(1))
    return pl.pallas_call(
        _c_out_ffn_body,
        grid=(bsz, t // tm),
        in_specs=[tok(d), rows, rows, tok(C_QW),
                  pl.BlockSpec((None, C_QW, d), lambda b, i: (li, 0, 0), **const),
                  pl.BlockSpec((None, None, d, 2 * D_FF), lambda b, i: (l, s, 0, 0), **const),
                  pl.BlockSpec((None, None, D_FF, d), lambda b, i: (l, s, 0, 0), **const)],
        out_specs=tok(d),
        out_shape=jax.ShapeDtypeStruct(x.shape, F32),
        compiler_params=_cparams(("parallel", "parallel"), 56),
        name="c_out_ffn",
    )(x, mod1, mod, a, w_out, wgu, wd)


def _rope_full(x, cos2, sin2):
    return x * cos2 + pltpu.roll(x, C_HD // 2, 1) * sin2


def _c_prep_body(*refs, rope):
    if rope:
        x_ref, mod_ref, w_ref, qn_ref, kn_ref, cos_ref, sin_ref, q_ref, k_ref, v_ref = refs
        cos2, sin2 = cos_ref[...], sin_ref[...]
    else:
        x_ref, mod_ref, w_ref, qn_ref, kn_ref, q_ref, k_ref, v_ref = refs
    x = x_ref[...]
    mod = mod_ref[...]
    h = (x * (1.0 + mod[1:2]) + mod[0:1]).astype(BF16)
    qkv = _dot(h, w_ref[...])
    qn = qn_ref[...] * (C_HD ** -0.5 * LOG2E)
    kn = kn_ref[...]

    heads = [(qkv[:, j * C_HD:(j + 1) * C_HD], qn) for j in range(C_HEADS)]
    heads += [(qkv[:, C_QW + j * C_HD:C_QW + (j + 1) * C_HD], kn) for j in range(C_KV)]
    ms = [jnp.mean(a * a, axis=-1, keepdims=True) for a, _ in heads]
    outs = [a * lax.rsqrt(m + NORM_EPS) * gain for (a, gain), m in zip(heads, ms)]
    if rope:
        outs = [_rope_full(a, cos2, sin2) for a in outs]
    for j in range(C_HEADS):
        q_ref[:, j * C_HD:(j + 1) * C_HD] = outs[j].astype(BF16)
    for j in range(C_KV):
        k_ref[:, j * C_HD:(j + 1) * C_HD] = outs[C_HEADS + j].astype(BF16)
    v_ref[...] = qkv[:, C_QW + C_KW:].astype(BF16)


def _c_prep(x, mod, w_in, l, qn, kn, tables):
    bsz, t, d = x.shape
    tm = _token_tile(t)
    per_batch = mod.shape[0] > 1
    rope = tables is not None
    in_specs = [pl.BlockSpec((None, tm, d), lambda b, i: (b, i, 0)),
                pl.BlockSpec((None, 8, d), lambda b, i: (b if per_batch else 0, 0, 0)),
                pl.BlockSpec((None, d, C_IN), lambda b, i: (l, 0, 0), pipeline_mode=pl.Buffered(1)),
                pl.BlockSpec((1, C_HD), lambda b, i: (0, 0)),
                pl.BlockSpec((1, C_HD), lambda b, i: (0, 0))]
    args = [x, mod, w_in, qn, kn]
    if rope:
        in_specs += [pl.BlockSpec((tm, C_HD), lambda b, i: (i, 0))] * 2
        args += list(tables)
    return pl.pallas_call(
        functools.partial(_c_prep_body, rope=rope),
        grid=(bsz, t // tm),
        in_specs=in_specs,
        out_specs=[pl.BlockSpec((None, tm, C_QW), lambda b, i: (b, i, 0)),
                   pl.BlockSpec((None, tm, C_KW), lambda b, i: (b, i, 0)),
                   pl.BlockSpec((None, tm, C_KW), lambda b, i: (b, i, 0))],
        out_shape=[jax.ShapeDtypeStruct((bsz, t, C_QW), BF16),
                   jax.ShapeDtypeStruct((bsz, t, C_KW), BF16),
                   jax.ShapeDtypeStruct((bsz, t, C_KW), BF16)],
        compiler_params=_cparams(("parallel", "parallel"), 40),
        name="c_prep",
    )(*args)


C_GROUP = C_HEADS // C_KV
TQ_C = 128
KC_C = 256
RB_C = 64
FLASH_UNROLL = 4
LOG2E = 1.4426950408889634


def _flash_body(q_ref, k_ref, v_ref, o_ref, q_scr, s_scr, p0_scr, p1_scr, m_scr, mb_scr, acc_scr, *,
                nch, tail_keys):
    i = pl.program_id(2)
    slot_a = i % 2
    slot_b = 1 - slot_a
    rows = q_scr.shape[0]
    tq = q_ref.shape[0]

    @pl.when(i == 0)
    def _():
        s_scr[1] = jnp.zeros(s_scr.shape[1:], F32)
        m_scr[1] = jnp.zeros(m_scr.shape[1:], F32)

    for g in range(C_GROUP):
        q_scr[g * tq:(g + 1) * tq, :] = q_ref[:, g * C_HD:(g + 1) * C_HD]
    m_cur = jnp.max(m_scr[slot_b], axis=-1, keepdims=True)
    mb_scr[...] = jnp.broadcast_to(m_cur, (rows, LANE))
    m_scr[slot_a] = jnp.full((rows, LANE), NEG, F32)
    acc_scr[...] = jnp.zeros((rows, 2 * C_HD), F32)

    def probs(j, p_scr):
        for r0 in range(0, rows, RB_C):
            mb = mb_scr[r0:r0 + RB_C, :]
            for c0 in range(0, KC_C, LANE):
                p = jnp.exp2(s_scr[slot_b, j, r0:r0 + RB_C, c0:c0 + LANE] - mb)
                p_scr[r0:r0 + RB_C, c0:c0 + LANE] = p.astype(BF16)

    def pv(j, p_scr):
        k0 = pl.multiple_of(j * KC_C, KC_C)
        acc_scr[...] += _dot(p_scr[...], v_ref[pl.ds(k0, KC_C), :])

    def logits(j, n_keys=KC_C):
        k0 = pl.multiple_of(j * KC_C, KC_C)
        s = _dot_nt(q_scr[...], k_ref[pl.ds(k0, KC_C), :])
        if n_keys < KC_C:
            s = jnp.where(lax.broadcasted_iota(jnp.int32, (1, KC_C), 1) < n_keys, s, NEG)
        s_scr[slot_a, j] = s
        mt = s[:, 0:LANE]
        for c0 in range(LANE, KC_C, LANE):
            mt = jnp.maximum(mt, s[:, c0:c0 + LANE])
        m_scr[slot_a] = jnp.maximum(m_scr[slot_a], mt)

    probs(0, p0_scr)

    def pair(t, carry):
        j = 2 * t
        logits(j)
        pv(j, p0_scr)
        probs(j + 1, p1_scr)
        logits(j + 1)
        pv(j + 1, p1_scr)
        probs(j + 2, p0_scr)
        return carry

    lax.fori_loop(0, (nch - 1) // 2, pair, 0, unroll=FLASH_UNROLL)
    pv(nch - 1, p0_scr)
    logits(nch - 1, tail_keys)

    acc = acc_scr[...]
    o = acc[:, :C_HD] / acc[:, C_HD:C_HD + 1]
    for g in range(C_GROUP):
        o_ref[:, g * C_HD:(g + 1) * C_HD] = o[g * tq:(g + 1) * tq].astype(BF16)


def _flash(q, ks, vs):
    bsz, t, _ = q.shape
    n_keys = sum(a.shape[1] for a in ks)
    nch = -(-n_keys // KC_C)
    nch += 1 - nch % 2
    n = nch * KC_C
    tail_keys = n_keys - (nch - 1) * KC_C
    assert 0 < tail_keys <= KC_C
    pad = [jnp.zeros((bsz, n - n_keys, C_KW), BF16)] if n > n_keys else []
    k = jnp.concatenate(list(ks) + pad, axis=1)
    ones = jnp.broadcast_to((jnp.arange(C_HD) == 0).astype(BF16), (bsz, n, C_HD))
    v = jnp.concatenate(
        [jnp.concatenate([a[:, :, j * C_HD:(j + 1) * C_HD] for a in list(vs) + pad], axis=1) if i == 0 else ones
         for j in range(C_KV) for i in range(2)], axis=2)
    tq = TQ_C
    nq = t // tq
    rows = C_GROUP * tq
    gw = C_GROUP * C_HD
    return pl.pallas_call(
        functools.partial(_flash_body, nch=nch, tail_keys=tail_keys),
        grid=(bsz, C_KV, nq + 1),
        in_specs=[pl.BlockSpec((None, tq, gw), lambda b, h, i: (b, jnp.minimum(i, nq - 1), h)),
                  pl.BlockSpec((None, n, C_HD), lambda b, h, i: (b, 0, h)),
                  pl.BlockSpec((None, n, 2 * C_HD), lambda b, h, i: (b, 0, h))],
        out_specs=pl.BlockSpec((None, tq, gw), lambda b, h, i: (b, jnp.maximum(i - 1, 0), h)),
        out_shape=jax.ShapeDtypeStruct((bsz, t, C_QW), BF16),
        scratch_shapes=[pltpu.VMEM((rows, C_HD), BF16),
                        pltpu.VMEM((2, nch, rows, KC_C), F32),
                        pltpu.VMEM((rows, KC_C), BF16),
                        pltpu.VMEM((rows, KC_C), BF16),
                        pltpu.VMEM((2, rows, LANE), F32),
                        pltpu.VMEM((rows, LANE), F32),
                        pltpu.VMEM((rows, 2 * C_HD), F32)],
        compiler_params=_cparams(("parallel", "parallel", "arbitrary"), 48),
        name="flash",
    )(q, k, v)


def _c_out_body(x_ref, mod_ref, o_ref, w_ref, out_ref):
    mod = mod_ref[...]
    y = _dot(o_ref[...], w_ref[...])
    out_ref[...] = _res_ln(x_ref[...], y, mod[2:3], mod[3:4], mod[4:5])


def _c_out(x, mod, o, w_out, l):
    bsz, t, d = x.shape
    tm = _token_tile(t)
    per_batch = mod.shape[0] > 1
    return pl.pallas_call(
        _c_out_body,
        grid=(bsz, t // tm),
        in_specs=[pl.BlockSpec((None, tm, d), lambda b, i: (b, i, 0)),
                  pl.BlockSpec((None, 8, d), lambda b, i: (b if per_batch else 0, 0, 0)),
                  pl.BlockSpec((None, tm, C_QW), lambda b, i: (b, i, 0)),
                  pl.BlockSpec((None, C_QW, d), lambda b, i: (l, 0, 0), pipeline_mode=pl.Buffered(1))],
        out_specs=pl.BlockSpec((None, tm, d), lambda b, i: (b, i, 0)),
        out_shape=jax.ShapeDtypeStruct(x.shape, F32),
        compiler_params=_cparams(("parallel", "parallel"), 32),
        name="c_out",
    )(x, mod, o, w_out)


def _rope_half_tile(x, cos2, sin2, lane_lo):
    half = A_HD // 2
    partner = jnp.where(lane_lo, pltpu.roll(x, LANE - half, 1), pltpu.roll(x, half, 1))
    return x * cos2 + partner * sin2


def _gate_act(raw, is_beta, a_log, dt_bias):
    sp_in = raw + dt_bias
    softplus = jnp.maximum(sp_in, 0.0) + jnp.log(1.0 + jnp.exp(-jnp.abs(sp_in)))
    return jnp.where(is_beta, jax.nn.sigmoid(raw), -jnp.exp(a_log) * softplus)


def _ab_prep_body(*refs, rope, tm):
    if rope:
        (xp_ref, x_ref, xn_ref, mod_ref, w_ref, cw_ref, gp_ref, cos_ref, sin_ref,
         aq_ref, ak_ref, av_ref, bq_ref, bk_ref, bv_ref, z_ref, gc_ref, gr_ref, cbuf) = refs
    else:
        (xp_ref, x_ref, xn_ref, mod_ref, w_ref, cw_ref, gp_ref,
         aq_ref, ak_ref, av_ref, bq_ref, bk_ref, bv_ref, z_ref, gc_ref, gr_ref, cbuf) = refs
    i = pl.program_id(1)
    nt = pl.num_programs(1)
    mod = mod_ref[...]
    scale1, shift = 1.0 + mod[1:2], mod[0:1]
    h = (x_ref[...] * scale1 + shift).astype(BF16)
    halo = jnp.concatenate([xp_ref[...], xn_ref[...]], axis=0)
    h_all = jnp.concatenate([h, (halo * scale1 + shift).astype(BF16)], axis=0)
    proj_all = _dot(h_all, w_ref[...])
    proj = proj_all[:tm]

    if rope:
        cos2, sin2 = cos_ref[...], sin_ref[...]
        lane_lo = (lax.broadcasted_iota(jnp.int32, (tm, LANE), 1) % A_HD) < (A_HD // 2)
    for j in range(A_Q // LANE):
        a = proj[:, j * LANE:(j + 1) * LANE]
        if rope:
            a = _rope_half_tile(a, cos2, sin2, lane_lo)
        aq_ref[:, j * LANE:(j + 1) * LANE] = (a * (A_HD ** -0.5 * LOG2E)).astype(BF16)
    a = proj[:, OFF_AK:OFF_AK + LANE]
    if rope:
        a = _rope_half_tile(a, cos2, sin2, lane_lo)
    ak_ref[...] = a.astype(BF16)
    av_ref[...] = proj[:, OFF_AV:OFF_AV + LANE].astype(BF16)
    z_ref[...] = proj[:, OFF_BZ:OFF_BZ + B_VW]

    gp = gp_ref[...]
    lane = lax.broadcasted_iota(jnp.int32, (1, LANE), 1)
    gates = _gate_act(proj[:, OFF_GATE:OFF_GATE + LANE], lane < 2 * B_HEADS, gp[0:1], gp[1:2])
    gc_ref[...] = gates
    gr_ref[...] = gates.T[:N_GATE, :]

    cbuf[0:HALO, :] = proj_all[tm:tm + HALO, OFF_BQKV:OFF_BQKV + B_QKV] * jnp.where(i > 0, 1.0, 0.0)
    cbuf[HALO:HALO + tm, :] = proj[:, OFF_BQKV:OFF_BQKV + B_QKV]
    cbuf[HALO + tm:, :] = proj_all[tm + HALO:, OFF_BQKV:OFF_BQKV + B_QKV] * jnp.where(i < nt - 1, 1.0, 0.0)
    cw = cw_ref[...]
    pad = CONV_K // 2
    for part, dst in enumerate((bq_ref, bk_ref, bv_ref)):
        for hd in range(B_HEADS):
            c0 = part * B_QK + hd * B_DK
            acc = None
            for j in range(CONV_K):
                term = cbuf[HALO - pad + j:HALO - pad + j + tm, c0:c0 + B_DK] * cw[j:j + 1, c0:c0 + B_DK]
                acc = term if acc is None else acc + term
            acc = _silu(acc)
            if part < 2:
                acc = acc * lax.rsqrt(jnp.sum(acc * acc, axis=-1, keepdims=True) + NORM_EPS)
            if part == 0:
                acc = acc * (B_DK ** -0.5)
            dst[:, hd * B_DK:(hd + 1) * B_DK] = acc


def _ab_prep(x, mod, w_pad, conv_w, gate_p, l, tables):
    bsz, t, d = x.shape
    tm = _token_tile(t)
    nh = tm // HALO
    nblk = t // HALO
    per_batch = mod.shape[0] > 1
    rope = tables is not None
    const = dict(pipeline_mode=pl.Buffered(1))
    in_specs = [pl.BlockSpec((None, HALO, d), lambda b, i: (b, jnp.maximum(i * nh - 1, 0), 0)),
                pl.BlockSpec((None, tm, d), lambda b, i: (b, i, 0)),
                pl.BlockSpec((None, HALO, d), lambda b, i: (b, jnp.minimum((i + 1) * nh, nblk - 1), 0)),
                pl.BlockSpec((None, 8, d), lambda b, i: (b if per_batch else 0, 0, 0)),
                pl.BlockSpec((None, d, AB_IN_PAD), lambda b, i: (l, 0, 0), **const),
                pl.BlockSpec((None, 8, B_QKV), lambda b, i: (l, 0, 0)),
                pl.BlockSpec((None, 8, LANE), lambda b, i: (l, 0, 0))]
    args = [x, x, x, mod, w_pad, conv_w, gate_p]
    if rope:
        in_specs += [pl.BlockSpec((tm, LANE), lambda b, i: (i, 0))] * 2
        args += list(tables)
    tok = lambda w: pl.BlockSpec((None, tm, w), lambda b, i: (b, i, 0))
    shp = lambda w, dt: jax.ShapeDtypeStruct((bsz, t, w), dt)
    return pl.pallas_call(
        functools.partial(_ab_prep_body, rope=rope, tm=tm),
        grid=(bsz, t // tm),
        in_specs=in_specs,
        out_specs=[tok(A_Q), tok(A_KVW), tok(A_KVW), tok(B_QK), tok(B_QK), tok(B_VW), tok(B_VW), tok(LANE),
                   pl.BlockSpec((None, N_GATE, tm), lambda b, i: (b, 0, i))],
        out_shape=[shp(A_Q, BF16), shp(A_KVW, BF16), shp(A_KVW, BF16), shp(B_QK, F32), shp(B_QK, F32),
                   shp(B_VW, F32), shp(B_VW, F32), shp(LANE, F32),
                   jax.ShapeDtypeStruct((bsz, N_GATE, t), F32)],
        scratch_shapes=[pltpu.VMEM((tm + 2 * HALO, B_QKV), F32)],
        compiler_params=_cparams(("parallel", "parallel"), 52),
        name="ab_prep",
    )(*args)


A_GROUP = A_HEADS // A_KV


WIN_TILE = 1024
WIN_GROUP = 1
A_HEAD_PERM = tuple(h for t in range(A_GROUP) for h in (t, A_GROUP + t))


def _win_body(*refs, local, nqb):
    if local:
        q_ref, kp_ref, kc_ref, kn_ref, vp_ref, vc_ref, vn_ref, kx_ref, vx_ref, sink_ref, o_ref = refs
    else:
        q_ref, kx_ref, vx_ref, sink_ref, o_ref = refs
    lane = lax.broadcasted_iota(jnp.int32, (1, LANE), 1)
    kv_lanes = (lane < A_HD, lane >= A_HD)
    kx = kx_ref[...]
    vx = vx_ref[...]
    sink = sink_ref[...] * LOG2E
    if local:
        i = pl.program_id(1)
        nsteps = pl.num_programs(1)
        kcat = jnp.concatenate([kp_ref[...], kc_ref[...], kn_ref[...]], axis=0)
        vcat = jnp.concatenate([vp_ref[...], vc_ref[...], vn_ref[...]], axis=0)
        r = lax.broadcasted_iota(jnp.int32, (BLOCK, 3 * BLOCK), 0)
        c = lax.broadcasted_iota(jnp.int32, (BLOCK, 3 * BLOCK), 1)
        band = (c >= r) & (c <= r + 2 * WINDOW)
    zero = jnp.zeros((), BF16)
    sks = [jnp.concatenate([jnp.broadcast_to(sink[kv * A_GROUP + g:kv * A_GROUP + g + 1, 0:1], (BLOCK, 1))
                            for g in range(A_GROUP)], axis=0) for kv in range(A_KV)]
    def logits(qb0):
        units = []
        for qb in range(qb0, min(qb0 + WIN_GROUP, nqb)):
            r0 = qb * BLOCK
            tiles = [q_ref[r0:r0 + BLOCK, t * LANE:(t + 1) * LANE] for t in range(A_GROUP)]
            for kv in range(A_KV):
                qs = jnp.concatenate([jnp.where(kv_lanes[kv], tl, zero) for tl in tiles], axis=0)
                units.append(dict(qb=qb, kv=kv, r0=r0, qs=qs))
        for un in units:
            un["s_x"] = _dot_nt(un["qs"], kx)
            if local:
                un["s_l"] = _dot_nt(un["qs"], kcat[un["r0"]:un["r0"] + 3 * BLOCK])
        return units

    def finish(units):
        for un in units:
            m = jnp.maximum(jnp.max(un["s_x"], axis=-1, keepdims=True), sks[un["kv"]])
            if local:
                valid = band
                if un["qb"] == 0:
                    valid = valid & ((c >= BLOCK) | (i > 0))
                if un["qb"] == nqb - 1:
                    valid = valid & ((c < 2 * BLOCK) | (i < nsteps - 1))
                un["s_l"] = jnp.concatenate(
                    [jnp.where(valid, un["s_l"][g * BLOCK:(g + 1) * BLOCK], NEG) for g in range(A_GROUP)], axis=0)
                m = jnp.maximum(m, jnp.max(un["s_l"], axis=-1, keepdims=True))
            un["m"] = m
        for un in units:
            p_x = jnp.exp2(un["s_x"] - un["m"])
            un["den"] = jnp.sum(p_x, axis=-1, keepdims=True) + jnp.exp2(sks[un["kv"]] - un["m"])
            un["p_x"] = p_x.astype(BF16)
            if local:
                p_l = jnp.exp2(un["s_l"] - un["m"])
                un["den"] = un["den"] + jnp.sum(p_l, axis=-1, keepdims=True)
                un["p_l"] = p_l.astype(BF16)
        for un in units:
            o = _dot(un["p_x"], vx)
            if local:
                o = o + _dot(un["p_l"], vcat[un["r0"]:un["r0"] + 3 * BLOCK])
            un["o"] = o / un["den"]
        for u0 in range(0, len(units), A_KV):
            r0 = units[u0]["r0"]
            for t in range(A_GROUP):
                o_ref[r0:r0 + BLOCK, t * LANE:(t + 1) * LANE] = jnp.where(
                    kv_lanes[0], units[u0]["o"][t * BLOCK:(t + 1) * BLOCK],
                    units[u0 + 1]["o"][t * BLOCK:(t + 1) * BLOCK]).astype(BF16)

    starts = list(range(0, nqb, WIN_GROUP))
    pending = logits(starts[0])
    for nxt in starts[1:] + [None]:
        ahead = logits(nxt) if nxt is not None else None
        finish(pending)
        pending = ahead


def _win_attn(q, kx, vx, sink, l, k=None, v=None):
    bsz, t, _ = q.shape
    lc = kx.shape[1]
    local = k is not None
    tq = min(WIN_TILE, t)
    nqb = tq // BLOCK
    nb = t // BLOCK
    edge = lambda f: pl.BlockSpec((None, BLOCK, A_KVW), f)
    in_specs = [pl.BlockSpec((None, tq, A_Q), lambda b, i: (b, i, 0))]
    args = [q]
    if local:
        prv = lambda b, i: (b, jnp.maximum(i * nqb - 1, 0), 0)
        nxt = lambda b, i: (b, jnp.minimum((i + 1) * nqb, nb - 1), 0)
        cur = pl.BlockSpec((None, tq, A_KVW), lambda b, i: (b, i, 0))
        in_specs += [edge(prv), cur, edge(nxt), edge(prv), cur, edge(nxt)]
        args += [k, k, k, v, v, v]
    in_specs += [pl.BlockSpec((None, lc, A_KVW), lambda b, i: (b, 0, 0)),
                 pl.BlockSpec((None, lc, A_KVW), lambda b, i: (b, 0, 0)),
                 pl.BlockSpec((None, A_HEADS, LANE), lambda b, i: (l, 0, 0))]
    args += [kx, vx, sink]
    return pl.pallas_call(
        functools.partial(_win_body, local=local, nqb=nqb),
        grid=(bsz, t // tq),
        in_specs=in_specs,
        out_specs=pl.BlockSpec((None, tq, A_Q), lambda b, i: (b, i, 0)),
        out_shape=jax.ShapeDtypeStruct((bsz, t, A_Q), BF16),
        compiler_params=_cparams(("parallel", "parallel"), 40),
        name="win_attn",
    )(*args)


DELTA_BLOCK = 4 * CHUNK
DELTA_CHUNKS = DELTA_BLOCK // CHUNK
N_CHAIN = 2 * B_HEADS


def _split_bf16(a):
    hi = a.astype(BF16)
    return hi, (a - hi.astype(F32)).astype(BF16)


def _delta_intra(units, eye):
    for un in units:
        decay = jnp.exp(jnp.where(un["incl"], un["gcc"] - un["gcr"], NEG))
        kb = un["k"] * un["beta"]
        a = _dot_nt(jnp.concatenate([kb, un["q"]], axis=0).astype(BF16), un["k"].astype(BF16))
        un["p"] = jnp.where(un["strict"], a[:CHUNK] * decay, 0.0) * -1.0
        un["attn"] = (a[CHUNK:] * decay).astype(BF16)
        un["t"] = eye + un["p"]
        eg = jnp.exp(un["gcc"])
        un["rhs"] = jnp.concatenate([un["v"] * un["beta"], kb * eg], axis=1).astype(BF16)
        un["qd"] = un["q"] * eg
        un["kt_t"] = (un["k"] * jnp.exp(un["gtot"] - un["gcc"])).T.astype(BF16)
        un["dec"] = jnp.exp(un["gtot"])
    for un in units:
        pb = un["p"].astype(BF16)
        un["pb"] = _dot(pb, pb).astype(BF16)
    for _ in range(CHUNK.bit_length() - 3):
        for un in units:
            both = _dot(jnp.concatenate([un["pb"], un["t"].astype(BF16)], axis=0), un["pb"])
            un["t"] = un["t"] + both[CHUNK:]
            un["pb"] = both[:CHUNK].astype(BF16)
    for un in units:
        un["t"] = un["t"] + _dot(un["t"].astype(BF16), un["pb"])
    for un in units:
        uw = _dot(un["t"].astype(BF16), un["rhs"])
        un["u"] = uw[:, :B_DV]
        un["wq"] = jnp.concatenate([uw[:, B_DV:], un["qd"]], axis=0).astype(BF16)


def _delta_body(qf_ref, kf_ref, vf_ref, gcf_ref, grf_ref, qb_ref, kb_ref, vb_ref, gcb_ref, grb_ref, s0_ref,
                of_ref, ob_ref, s_ref):
    @pl.when(pl.program_id(1) == 0)
    def _():
        s_ref[...] = s0_ref[...]

    li = lax.broadcasted_iota(jnp.int32, (CHUNK, CHUNK), 0)
    lj = lax.broadcasted_iota(jnp.int32, (CHUNK, CHUNK), 1)
    eye = jnp.where(li == lj, 1.0, 0.0)
    tri_l = jnp.where(li >= lj, 1.0, 0.0).astype(BF16)
    tri_u = jnp.where(li <= lj, 1.0, 0.0).astype(BF16)

    dirs = ((qf_ref, kf_ref, vf_ref, gcf_ref, grf_ref, of_ref, tuple(range(DELTA_CHUNKS)), tri_l, tri_u, li >= lj, li > lj, CHUNK - 1),
            (qb_ref, kb_ref, vb_ref, gcb_ref, grb_ref, ob_ref, tuple(reversed(range(DELTA_CHUNKS))), tri_u, tri_l, li <= lj, li < lj, 0))
    units = {}
    for d, (q_ref, k_ref, v_ref, gc_ref, gr_ref, o_ref, order, tri_c, tri_r, incl, strict, last) in enumerate(dirs):
        for c in order:
            r0 = c * CHUNK
            gcol = gc_ref[r0:r0 + CHUNK, :]
            ghi, glo = _split_bf16(gcol)
            cum_c = _dot(tri_c, ghi) + _dot(tri_c, glo)
            grow = gr_ref[:, r0:r0 + CHUNK]
            rhi, rlo = _split_bf16(grow)
            cum_r = _dot(rhi, tri_r) + _dot(rlo, tri_r)
            for hd in range(B_HEADS):
                col = d * B_HEADS + hd
                gl = 2 * B_HEADS + col
                lo, hi = hd * B_DK, (hd + 1) * B_DK
                units[(d, c, hd)] = dict(
                    q=q_ref[r0:r0 + CHUNK, lo:hi], k=k_ref[r0:r0 + CHUNK, lo:hi], v=v_ref[r0:r0 + CHUNK, lo:hi],
                    beta=gcol[:, col:col + 1], gcc=cum_c[:, gl:gl + 1], gcr=cum_r[gl:gl + 1, :],
                    gtot=cum_c[last:last + 1, gl:gl + 1], incl=incl, strict=strict)
    _delta_intra(list(units.values()), eye)

    chains = [(d, hd) for d in range(2) for hd in range(B_HEADS)]
    state = [s_ref[d * B_HEADS + hd] for d, hd in chains]
    for step in range(DELTA_CHUNKS):
        cur = [units[(d, dirs[d][6][step], hd)] for d, hd in chains]
        ws = [_dot(un["wq"], s.astype(BF16)) for un, s in zip(cur, state)]
        v_new = [(un["u"] - w[:CHUNK]).astype(BF16) for un, w in zip(cur, ws)]
        outs = [w[CHUNK:] + _dot(un["attn"], vn) for un, w, vn in zip(cur, ws, v_new)]
        state = [s * un["dec"] + _dot(un["kt_t"], vn) for un, s, vn in zip(cur, state, v_new)]
        for (d, hd), o in zip(chains, outs):
            r0 = dirs[d][6][step] * CHUNK
            dirs[d][5][r0:r0 + CHUNK, hd * B_DV:(hd + 1) * B_DV] = o
    for (d, hd), s in zip(chains, state):
        s_ref[d * B_HEADS + hd] = s


def _delta(bq, bk, bv, gcol, grow, s0):
    bsz, t, _ = bq.shape
    ns = t // DELTA_BLOCK
    fwd = lambda w: pl.BlockSpec((None, DELTA_BLOCK, w), lambda b, s: (b, s, 0))
    bwd = lambda w: pl.BlockSpec((None, DELTA_BLOCK, w), lambda b, s: (b, ns - 1 - s, 0))
    st = pl.BlockSpec((None, N_CHAIN, B_DK, B_DV), lambda b, s: (b, 0, 0, 0))
    return pl.pallas_call(
        _delta_body,
        grid=(bsz, ns),
        in_specs=[fwd(B_QK), fwd(B_QK), fwd(B_VW), fwd(LANE),
                  pl.BlockSpec((None, N_GATE, DELTA_BLOCK), lambda b, s: (b, 0, s)),
                  bwd(B_QK), bwd(B_QK), bwd(B_VW), bwd(LANE),
                  pl.BlockSpec((None, N_GATE, DELTA_BLOCK), lambda b, s: (b, 0, ns - 1 - s)),
                  st],
        out_specs=[fwd(B_VW), bwd(B_VW), st],
        out_shape=[jax.ShapeDtypeStruct((bsz, t, B_VW), F32), jax.ShapeDtypeStruct((bsz, t, B_VW), F32),
                   jax.ShapeDtypeStruct((bsz, N_CHAIN, B_DK, B_DV), F32)],
        compiler_params=_cparams(("parallel", "arbitrary"), 32),
        name="delta",
    )(bq, bk, bv, gcol, grow, bq, bk, bv, gcol, grow, s0)


def _ab_out_body(x_ref, mod_ref, oa_ref, of_ref, ob_ref, z_ref, gn_ref, w_ref, out_ref):
    mod = mod_ref[...]
    gn = gn_ref[...]
    y = _dot(oa_ref[...], w_ref[0:A_Q, :])
    for hd in range(B_HEADS):
        lo, hi = hd * B_DV, (hd + 1) * B_DV
        o = of_ref[:, lo:hi] + ob_ref[:, lo:hi]
        o = o * lax.rsqrt(jnp.mean(o * o, axis=-1, keepdims=True) + NORM_EPS) * gn
        o = (o * _silu(z_ref[:, lo:hi])).astype(BF16)
        y = y + _dot(o, w_ref[A_Q + lo:A_Q + hi, :])
    out_ref[...] = _res_ln(x_ref[...], y, mod[2:3], mod[3:4], mod[4:5])


def _ab_out(x, mod, oa, of, ob, z, gnorm, w_out, l):
    bsz, t, d = x.shape
    tm = _token_tile(t)
    per_batch = mod.shape[0] > 1
    tok = lambda w: pl.BlockSpec((None, tm, w), lambda b, i: (b, i, 0))
    return pl.pallas_call(
        _ab_out_body,
        grid=(bsz, t // tm),
        in_specs=[tok(d),
                  pl.BlockSpec((None, 8, d), lambda b, i: (b if per_batch else 0, 0, 0)),
                  tok(A_Q), tok(B_VW), tok(B_VW), tok(B_VW),
                  pl.BlockSpec((None, 1, B_DV), lambda b, i: (l, 0, 0)),
                  pl.BlockSpec((None, A_Q + B_VW, d), lambda b, i: (l, 0, 0), pipeline_mode=pl.Buffered(1))],
        out_specs=tok(d),
        out_shape=jax.ShapeDtypeStruct(x.shape, F32),
        compiler_params=_cparams(("parallel", "parallel"), 40),
        name="ab_out",
    )(x, mod, oa, of, ob, z, gnorm, w_out)


def _rope_tables(rows, head_dim):
    n_freq = head_dim // 4
    inv = ROPE_THETA ** (-jnp.arange(n_freq, dtype=F32) / n_freq)
    r, col = jnp.meshgrid(jnp.arange(rows, dtype=F32), jnp.arange(GRID_W, dtype=F32), indexing='ij')
    r, col = r.reshape(-1), col.reshape(-1)
    ang = jnp.concatenate([r[:, None] * inv, col[:, None] * inv], axis=-1)
    cos, sin = jnp.cos(ang), jnp.sin(ang)
    cos2 = jnp.concatenate([cos, cos], axis=-1)
    sin2 = jnp.concatenate([-sin, sin], axis=-1)
    rep = LANE // head_dim
    return jnp.tile(cos2, (1, rep)), jnp.tile(sin2, (1, rep))


def _pad_rows(a, rows):
    return jnp.pad(a, ((0, 0), (0, rows - a.shape[1]), (0, 0)))


def kernel(x, c, ctx, c_ctx, ada_w, ada_b, ln_g, ln_b, ffn_w_gu, ffn_w_down, ab_w_in, ab_conv_w, ab_a_log,
           ab_dt_bias, ab_gnorm, ab_sink, ab_w_out, c_w_in, c_q_norm, c_k_norm, c_w_out):
    bsz, t, d = x.shape
    depth = ada_w.shape[0]
    n_even = ab_w_in.shape[0]
    rows = t // GRID_W
    tab_a = _rope_tables(rows, A_HD)
    tab_c = _rope_tables(rows, C_HD)

    wgu = ffn_w_gu.astype(BF16)
    wd = ffn_w_down.astype(BF16)
    head_cols = jnp.asarray([h * A_HD + j for h in A_HEAD_PERM for j in range(A_HD)], jnp.int32)
    ab_w_in_p = jnp.concatenate([jnp.take(ab_w_in[:, :, :A_Q], head_cols, axis=2), ab_w_in[:, :, A_Q:]], axis=2)
    ab_w_pad = jnp.pad(ab_w_in_p, ((0, 0), (0, 0), (0, AB_IN_PAD - AB_IN))).astype(BF16)
    ab_wo = jnp.concatenate([jnp.take(ab_w_out[:, :A_Q], head_cols, axis=1), ab_w_out[:, A_Q:]], axis=1).astype(BF16)
    c_wi = c_w_in.astype(BF16)
    c_wo = c_w_out.astype(BF16)
    conv_w = _pad_rows(ab_conv_w, 8)
    zeros8 = jnp.zeros((n_even, 2 * B_HEADS), F32)
    a_log16 = jnp.concatenate([zeros8, ab_a_log.reshape(n_even, 2 * B_HEADS)], axis=1)
    dtb16 = jnp.concatenate([zeros8, ab_dt_bias.reshape(n_even, 2 * B_HEADS)], axis=1)
    gate_p = _pad_rows(jnp.pad(jnp.stack([a_log16, dtb16], axis=1), ((0, 0), (0, 0), (0, LANE - N_GATE))), 8)
    sink = jnp.broadcast_to(ab_sink[:, :, None], (n_even, A_HEADS, LANE))
    gnorm = ab_gnorm.reshape(n_even, 1, B_DV)
    qn = c_q_norm.reshape(-1, 1, C_HD)
    kn = c_k_norm.reshape(-1, 1, C_HD)

    nrow = ((bsz + 1 + 7) // 8) * 8
    cvec = jnp.pad(jnp.concatenate([c, c_ctx[None]], axis=0), ((0, nrow - bsz - 1), (0, 0)))
    m = _modulation(cvec, ada_w, ada_b).reshape(depth, nrow, N_MOD, d)

    def mod_rows(l, s):
        ln = jnp.stack([ln_g[l, s], ln_b[l, s]], axis=0)
        rows_l = jnp.concatenate([m[l, :bsz, 3 * s:3 * s + 3], jnp.broadcast_to(ln, (bsz, 2, d)),
                                  jnp.zeros((bsz, 3, d), F32)], axis=1)
        rows_c = jnp.concatenate([m[l, bsz:bsz + 1, 3 * s:3 * s + 3], ln[None],
                                  jnp.zeros((1, 3, d), F32)], axis=1)
        return rows_l, rows_c

    xl, xc = x, ctx
    for l in range(depth):
        ctx_out = l < depth - 1
        i = l // 2
        m0l, m0c = mod_rows(l, 0)
        m1l, m1c = mod_rows(l, 1)
        m2l, m2c = mod_rows(l, 2)
        xl = _ffn(xl, m0l, wgu, wd, l, 0)
        xc = _ffn(xc, m0c, wgu, wd, l, 0)
        if l % 2 == 0:
            pc = _ab_prep(xc, m1c, ab_w_pad, conv_w, gate_p, i, None)
            pl_ = _ab_prep(xl, m1l, ab_w_pad, conv_w, gate_p, i, tab_a)
            aqc, akc, avc, bqc, bkc, bvc, zc, gcc, grc = pc
            aql, akl, avl, bql, bkl, bvl, zl, gcl, grl = pl_
            ol_a = _win_attn(aql, akc, avc, sink, i, akl, avl)
            s0 = jnp.zeros((bsz, N_CHAIN, B_DK, B_DV), F32)
            oc_f, oc_b, s_ctx = _delta(bqc, bkc, bvc, gcc, grc, s0)
            ol_f, ol_b, _ = _delta(bql, bkl, bvl, gcl, grl, s_ctx)
            xl = _ab_out(xl, m1l, ol_a, ol_f, ol_b, zl, gnorm, ab_wo, i)
            if ctx_out:
                oc_a = _win_attn(aqc, akc, avc, sink, i)
                xc = _ab_out(xc, m1c, oc_a, oc_f, oc_b, zc, gnorm, ab_wo, i)
        else:
            qc, kc, vc = _c_prep(xc, m1c, c_wi, i, qn[i], kn[i], None)
            ql, kl, vl = _c_prep(xl, m1l, c_wi, i, qn[i], kn[i], tab_c)
            ol = _flash(ql, [kc, kl], [vc, vl])
            xl = _c_out_ffn(xl, m1l, m2l, ol, c_wo, i, wgu, wd, l, 1)
            if ctx_out:
                oc = _flash(qc, [kc], [vc])
                xc = _c_out_ffn(xc, m1c, m2c, oc, c_wo, i, wgu, wd, l, 1)
            continue
        xl = _ffn(xl, m2l, wgu, wd, l, 1)
        if ctx_out:
            xc = _ffn(xc, m2c, wgu, wd, l, 1)
    return xl
```
